```python
import math
import jax, jax.numpy as jnp
from jax import lax
import numpy as np

D_MODEL = 1024
BATCH = 16
SEQ = 2048
DEPTH = 2

N_META = 16
BLOCK = 128
PAD = BLOCK - N_META

HEAD_DIM = 64
A_WIDTH = D_MODEL // 4
A_HEADS = A_WIDTH // HEAD_DIM
KV_RANK = 128
IDX_HEADS = 4
IDX_DIM = 64
TOPK_MAX = 256

B_WIDTH = D_MODEL // 2
S5_GROUP = 16
S5_GROUPS = B_WIDTH // S5_GROUP
S5_STATE = 64

C_WIDTH = D_MODEL // 4
C_HEADS = C_WIDTH // HEAD_DIM

MIX_WIDTH = A_WIDTH + B_WIDTH + C_WIDTH
D_FF = 4 * D_MODEL

NUM_BUCKETS = 32
MAX_DISTANCE = 128

RMS_EPS = 1e-6
NEG_INF = -1e30

PROJ_SIZES = (A_WIDTH, KV_RANK, IDX_HEADS * IDX_DIM, IDX_DIM, IDX_HEADS,
              B_WIDTH, C_WIDTH, C_WIDTH, C_WIDTH)
N_IN = sum(PROJ_SIZES)
SPLIT_POINTS = tuple(int(s) for s in np.cumsum(PROJ_SIZES)[:-1])

kernel_name = "hymba_dsa_s5_stickbreak_hybrid"


def rms_norm(x, g):
    x32 = x.astype(jnp.float32)
    y = x32 * lax.rsqrt(jnp.mean(x32 * x32, axis=-1, keepdims=True) + RMS_EPS)
    return (y * g.astype(jnp.float32)).astype(x.dtype)


def t5_bucket(n):
    max_exact = NUM_BUCKETS // 2
    n = jnp.maximum(n, 0)
    nf = jnp.maximum(n, max_exact).astype(jnp.float32)
    large = max_exact + (jnp.log(nf / max_exact) / math.log(MAX_DISTANCE / max_exact)
                         * (NUM_BUCKETS - max_exact)).astype(jnp.int32)
    large = jnp.minimum(large, NUM_BUCKETS - 1)
    return jnp.where(n < max_exact, n, large)


def dsa_attention(q_lat, ckv, iq, ik, iw, rel_bias, w_uv, k_top):
    bsz, plen = ckv.shape[:2]
    n_blocks = plen // BLOCK
    key_pos = jnp.arange(plen)

    def one_block(i):
        start = i * BLOCK
        t = start + jnp.arange(BLOCK)
        qb = lax.dynamic_slice_in_dim(q_lat, start, BLOCK, axis=1)
        iqb = lax.dynamic_slice_in_dim(iq, start, BLOCK, axis=1)
        iwb = lax.dynamic_slice_in_dim(iw, start, BLOCK, axis=1)
        idx_logits = jnp.einsum('bqhd,bkd->bqhk', iqb, ik).astype(jnp.float32) * IDX_DIM ** -0.5
        score = jnp.einsum('bqhk,bqh->bqk', jax.nn.relu(idx_logits),
                           iwb.astype(jnp.float32)) * IDX_HEADS ** -0.5
        admissible = (key_pos[None, :] <= t[:, None]) & (key_pos[None, :] >= PAD)
        score = jnp.where(admissible[None], score, NEG_INF)
        _, idx = lax.top_k(score, k_top)
        valid = (idx <= t[None, :, None]) & (idx >= PAD)
        c_sel = jax.vmap(lambda c, j: c[j])(ckv, idx)
        bias = rel_bias[t5_bucket(t[None, :, None] - idx)]
        logits = (jnp.einsum('bqhr,bqkr->bqhk', qb, c_sel).astype(jnp.float32)
                  + jnp.moveaxis(bias, -1, 2).astype(jnp.float32))
        logits = jnp.where(valid[:, :, None, :], logits, NEG_INF)
        p = jax.nn.softmax(logits, axis=-1).astype(ckv.dtype)
        o_lat = jnp.einsum('bqhk,bqkr->bqhr', p, c_sel)
        return jnp.einsum('bqhr,rhd->bqhd', o_lat, w_uv).reshape(bsz, BLOCK, -1)

    out = lax.map(one_block, jnp.arange(n_blocks))
    return jnp.moveaxis(out, 0, 1).reshape(bsz, plen, -1)


def stick_breaking_attention(q, k, v):
    bsz, plen = q.shape[:2]
    n_blocks = plen // BLOCK
    key_pos = jnp.arange(plen)

    def one_block(i):
        start = i * BLOCK
        t = start + jnp.arange(BLOCK)
        qb = lax.dynamic_slice_in_dim(q, start, BLOCK, axis=1)
        z = jnp.einsum('bqhd,bkhd->bhqk', qb, k).astype(jnp.float32) * HEAD_DIM ** -0.5
        strict = (key_pos[None, :] < t[:, None]) & (key_pos[None, :] >= PAD)
        log_keep = jnp.where(strict, jax.nn.log_sigmoid(-z), 0.0)
        later = lax.cumsum(log_keep, axis=3, reverse=True) - log_keep
        w = jnp.where(strict, jnp.exp(jax.nn.log_sigmoid(z) + later), 0.0).astype(v.dtype)
        return jnp.einsum('bhqk,bkhd->bqhd', w, v).reshape(bsz, BLOCK, -1)

    out = lax.map(one_block, jnp.arange(n_blocks))
    return jnp.moveaxis(out, 0, 1).reshape(bsz, plen, -1)


def _complex_linear_combine(e1, e2):
    a1r, a1i, b1r, b1i = e1
    a2r, a2i, b2r, b2i = e2
    return (a2r * a1r - a2i * a1i,
            a2r * a1i + a2i * a1r,
            a2r * b1r - a2i * b1i + b2r,
            a2r * b1i + a2i * b1r + b2i)


def s5_mixer(u, lam_re, lam_im, log_dt, b_re, b_im, c_re, c_im, d_skip, w_glu):
    bsz, seqlen, _ = u.shape
    u32 = u.astype(jnp.float32)
    ug = u32.reshape(bsz, seqlen, S5_GROUPS, S5_GROUP)
    dt = jnp.exp(log_dt.astype(jnp.float32))[:, None]
    lr = lam_re.astype(jnp.float32)
    li = lam_im.astype(jnp.float32)
    mag = jnp.exp(lr * dt)
    ar = mag * jnp.cos(li * dt)
    ai = mag * jnp.sin(li * dt)
    den = lr * lr + li * li
    nr, ni = ar - 1.0, ai
    fr = (nr * lr + ni * li) / den
    fi = (ni * lr - nr * li) / den
    bu_r0 = jnp.einsum('blgc,gpc->blgp', ug, b_re.astype(jnp.float32))
    bu_i0 = jnp.einsum('blgc,gpc->blgp', ug, b_im.astype(jnp.float32))
    bu_r = fr * bu_r0 - fi * bu_i0
    bu_i = fr * bu_i0 + fi * bu_r0
    a_r = jnp.broadcast_to(ar, (seqlen,) + ar.shape)
    a_i = jnp.broadcast_to(ai, (seqlen,) + ai.shape)

    def scan_one(br, bi):
        _, _, hr, hi = lax.associative_scan(_complex_linear_combine, (a_r, a_i, br, bi), axis=0)
        return hr, hi

    h_r, h_i = jax.vmap(scan_one)(bu_r, bu_i)
    y = (jnp.einsum('blgp,gcp->blgc', h_r, c_re.astype(jnp.float32))
         - jnp.einsum('blgp,gcp->blgc', h_i, c_im.astype(jnp.float32)))
    y = y.reshape(bsz, seqlen, B_WIDTH) + d_skip.astype(jnp.float32) * u32
    y = jax.nn.gelu(y)
    y = y * jax.nn.sigmoid(y @ w_glu.astype(jnp.float32))
    return y.astype(u.dtype)


def setup_inputs(seed: int = 0) -> dict:
    key = jax.random.key(seed)
    ks = jax.random.split(key, 24)
    f32 = jnp.float32
    nrm = lambda k, shape, s: jax.random.normal(k, shape, f32) * s
    gain = lambda k, shape: 1.0 + 0.02 * jax.random.normal(k, shape, f32)
    lam_im = jnp.broadcast_to(jnp.pi * jnp.arange(S5_STATE, dtype=f32), (DEPTH, S5_GROUPS, S5_STATE))
    return {
        "x": nrm(ks[0], (BATCH, SEQ, D_MODEL), 1.0),
        "meta_tokens": nrm(ks[1], (N_META, D_MODEL), 1.0),
        "rel_bias": nrm(ks[2], (NUM_BUCKETS, A_HEADS), 0.5),
        "norm1_g": gain(ks[3], (DEPTH, D_MODEL)),
        "w_in": nrm(ks[4], (DEPTH, D_MODEL, N_IN), D_MODEL ** -0.5),
        "kv_norm_g": gain(ks[5], (DEPTH, KV_RANK)),
        "w_uk": nrm(ks[6], (DEPTH, KV_RANK, A_HEADS, HEAD_DIM), KV_RANK ** -0.5),
        "w_uv": nrm(ks[7], (DEPTH, KV_RANK, A_HEADS, HEAD_DIM), KV_RANK ** -0.5),
        "lambda_re": -0.5 + nrm(ks[8], (DEPTH, S5_GROUPS, S5_STATE), 0.01),
        "lambda_im": lam_im + 0.0,
        "log_dt": jax.random.uniform(ks[9], (DEPTH, S5_GROUPS), f32, math.log(1e-3), math.log(1e-1)),
        "b_re": nrm(ks[10], (DEPTH, S5_GROUPS, S5_STATE, S5_GROUP), (2 * S5_GROUP) ** -0.5),
        "b_im": nrm(ks[11], (DEPTH, S5_GROUPS, S5_STATE, S5_GROUP), (2 * S5_GROUP) ** -0.5),
        "c_re": nrm(ks[12], (DEPTH, S5_GROUPS, S5_GROUP, S5_STATE), (2 * S5_STATE) ** -0.5),
        "c_im": nrm(ks[13], (DEPTH, S5_GROUPS, S5_GROUP, S5_STATE), (2 * S5_STATE) ** -0.5),
        "d_skip": nrm(ks[14], (DEPTH, B_WIDTH), 1.0),
        "w_glu": nrm(ks[15], (DEPTH, B_WIDTH, B_WIDTH), B_WIDTH ** -0.5),
        "gn_a": gain(ks[16], (DEPTH, A_WIDTH)),
        "gn_b": gain(ks[17], (DEPTH, B_WIDTH)),
        "gn_c": gain(ks[18], (DEPTH, C_WIDTH)),
        "w_out": nrm(ks[19], (DEPTH, MIX_WIDTH, D_MODEL), MIX_WIDTH ** -0.5),
        "norm2_g": gain(ks[20], (DEPTH, D_MODEL)),
        "w_mlp1": nrm(ks[21], (DEPTH, D_MODEL, D_FF), D_MODEL ** -0.5),
        "w_mlp2": nrm(ks[22], (DEPTH, D_FF, D_MODEL), D_FF ** -0.5),
        "final_g": gain(ks[23], (D_MODEL,)),
    }


def reference(x, meta_tokens, rel_bias, norm1_g, w_in, kv_norm_g, w_uk, w_uv, lambda_re,
              lambda_im, log_dt, b_re, b_im, c_re, c_im, d_skip, w_glu, gn_a, gn_b, gn_c,
              w_out, norm2_g, w_mlp1, w_mlp2, final_g):
    bsz, seq, _ = x.shape
    n_keys = seq + N_META
    k_top = min(TOPK_MAX, n_keys // 4)
    h = jnp.concatenate(
        [jnp.broadcast_to(meta_tokens.astype(x.dtype)[None], (bsz, N_META, D_MODEL)), x], axis=1)

    def pad_front(t):
        return jnp.pad(t, [(0, 0), (PAD, 0)] + [(0, 0)] * (t.ndim - 2))

    for l in range(DEPTH):
        hn = rms_norm(h, norm1_g[l])
        proj = hn @ w_in[l]
        qa, ckv, iq, ik, iw, ub, qc, kc, vc = jnp.split(proj, SPLIT_POINTS, axis=-1)

        ckv = rms_norm(ckv, kv_norm_g[l])
        qa = qa.reshape(bsz, n_keys, A_HEADS, HEAD_DIM)
        q_lat = jnp.einsum('blhd,rhd->blhr', qa, w_uk[l]) * HEAD_DIM ** -0.5
        ya = dsa_attention(pad_front(q_lat), pad_front(ckv),
                           pad_front(iq.reshape(bsz, n_keys, IDX_HEADS, IDX_DIM)),
                           pad_front(ik), pad_front(iw), rel_bias, w_uv[l], k_top)[:, PAD:]

        yb = s5_mixer(ub, lambda_re[l], lambda_im[l], log_dt[l], b_re[l], b_im[l],
                      c_re[l], c_im[l], d_skip[l], w_glu[l])

        hs = (bsz, n_keys, C_HEADS, HEAD_DIM)
        yc = stick_breaking_attention(pad_front(qc.reshape(hs)), pad_front(kc.reshape(hs)),
                                      pad_front(vc.reshape(hs)))[:, PAD:]

        y = jnp.concatenate([rms_norm(ya, gn_a[l]), rms_norm(yb, gn_b[l]),
                             rms_norm(yc, gn_c[l])], axis=-1)
        h = h + y @ w_out[l]

        hn = rms_norm(h, norm2_g[l])
        h = h + jnp.square(jax.nn.relu(hn @ w_mlp1[l])) @ w_mlp2[l]

    return rms_norm(h, final_g)[:, N_META:]
```

```python
import functools
import math

import jax
import jax.numpy as jnp
import numpy as np
from jax import lax
from jax.experimental import pallas as pl
from jax.experimental.pallas import tpu as pltpu

D_MODEL = 1024
N_META = 16
BLOCK = 128
PAD = BLOCK - N_META
HEAD_DIM = 64
A_WIDTH = 256
A_HEADS = 4
KV_RANK = 128
IDX_HEADS = 4
IDX_DIM = 64
TOPK_MAX = 256
B_WIDTH = 512
S5_GROUP = 16
S5_GROUPS = 32
S5_STATE = 64
C_WIDTH = 256
C_HEADS = 4
D_FF = 4 * D_MODEL
NUM_BUCKETS = 32
MAX_DISTANCE = 128
RMS_EPS = 1e-6
NEG_INF = -1e30

ROW_TILE = 512
S5_CHUNK = 32
S5_LANES = S5_GROUPS * S5_STATE
S5_SLABS = 4
BISECT_ITERS = 28
VMEM_LIMIT = 56 * 1024 * 1024

_BF = jnp.bfloat16
_F32 = jnp.float32

_C_QA = 0
_C_CKV = _C_QA + A_WIDTH
_C_IQ = _C_CKV + KV_RANK
_C_IKW = _C_IQ + IDX_HEADS * 128
_C_UB = _C_IKW + 128
_C_QC = _C_UB + B_WIDTH
_C_KC = _C_QC + C_WIDTH
_C_VC = _C_KC + C_WIDTH
_N_PROJ = _C_VC + C_WIDTH


def _rms(x, g):
    return x * lax.rsqrt(jnp.mean(x * x, axis=-1, keepdims=True) + RMS_EPS) * g


def _inproj_kernel(h_ref, g1_ref, w_ref, kvg_ref, wuk_ref,
                   qlat_ref, ckv_ref, iq_ref, ikw_ref, ub_ref, qc_ref, kc_ref, vc_ref):
    hn = _rms(h_ref[...], g1_ref[...]).astype(_BF)
    proj = jnp.dot(hn, w_ref[...], preferred_element_type=_F32)
    qa = proj[:, _C_QA:_C_QA + A_WIDTH].astype(_BF)
    qlat = jnp.dot(qa, wuk_ref[...], preferred_element_type=_F32) * (HEAD_DIM ** -0.5)
    for hh in range(A_HEADS):
        qlat_ref[hh] = qlat[:, hh * KV_RANK:(hh + 1) * KV_RANK].astype(_BF)
    ckv_ref[...] = _rms(proj[:, _C_CKV:_C_CKV + KV_RANK], kvg_ref[...]).astype(_BF)
    for hh in range(IDX_HEADS):
        c0 = _C_IQ + hh * 128
        iq_ref[hh] = proj[:, c0:c0 + IDX_DIM].astype(_BF)
    ikw_ref[...] = proj[:, _C_IKW:_C_IKW + 128]
    ub_ref[...] = proj[:, _C_UB:_C_UB + B_WIDTH].astype(_BF)
    qc_ref[...] = (proj[:, _C_QC:_C_QC + C_WIDTH] * (HEAD_DIM ** -0.5)).astype(_BF)
    kc_ref[...] = proj[:, _C_KC:_C_KC + C_WIDTH].astype(_BF)
    vc_ref[...] = proj[:, _C_VC:_C_VC + C_WIDTH].astype(_BF)


def _const_spec(shape):
    nd = len(shape)
    return pl.BlockSpec(shape, lambda *_: (0,) * nd, pipeline_mode=pl.Buffered(1))


def _inproj(h2d, g1, w_all, kvg, wuk_bd):
    rows = h2d.shape[0]
    tm = ROW_TILE
    row = lambda w: pl.BlockSpec((tm, w), lambda i: (i, 0))
    head = lambda w: pl.BlockSpec((4, tm, w), lambda i: (0, i, 0))
    out_shape = (
        jax.ShapeDtypeStruct((A_HEADS, rows, KV_RANK), _BF),
        jax.ShapeDtypeStruct((rows, KV_RANK), _BF),
        jax.ShapeDtypeStruct((IDX_HEADS, rows, IDX_DIM), _BF),
        jax.ShapeDtypeStruct((rows, 128), _F32),
        jax.ShapeDtypeStruct((rows, B_WIDTH), _BF),
        jax.ShapeDtypeStruct((rows, C_WIDTH), _BF),
        jax.ShapeDtypeStruct((rows, C_WIDTH), _BF),
        jax.ShapeDtypeStruct((rows, C_WIDTH), _BF),
    )
    return pl.pallas_call(
        _inproj_kernel,
        grid=(rows // tm,),
        in_specs=[row(D_MODEL), _const_spec((1, D_MODEL)), _const_spec((D_MODEL, _N_PROJ)),
                  _const_spec((1, KV_RANK)), _const_spec((A_WIDTH, A_HEADS * KV_RANK))],
        out_specs=(head(KV_RANK), row(KV_RANK), head(IDX_DIM), row(128), row(B_WIDTH),
                   row(C_WIDTH), row(C_WIDTH), row(C_WIDTH)),
        out_shape=out_shape,
        compiler_params=pltpu.CompilerParams(dimension_semantics=("arbitrary",),
                                             vmem_limit_bytes=VMEM_LIMIT),
        name="inproj",
    )(h2d, g1, w_all, kvg, wuk_bd)


def _dsa_kernel(k_top, relb_ref, qlat_ref, iq_ref, ikw_ref, ckv_ref, bucket_ref, wuv_ref, tri_ref,
                ya_ref, score_ref, bias_ref):
    i = pl.program_id(1)
    nkb = i + 1
    kf = float(k_top)

    @pl.when(i == 0)
    def _():
        for dd in range(2):
            bk = bucket_ref[dd]
            for hh in range(A_HEADS):
                acc = jnp.zeros((BLOCK, BLOCK), _F32)
                for nb in range(NUM_BUCKETS):
                    acc = jnp.where(bk == nb, relb_ref[nb, hh], acc)
                bias_ref[dd, hh] = acc
        for hh in range(A_HEADS):
            bias_ref[2, hh] = jnp.full((BLOCK, BLOCK), relb_ref[NUM_BUCKETS - 1, hh], _F32)

    row_t = i * BLOCK + lax.broadcasted_iota(jnp.int32, (BLOCK, BLOCK), 0)
    lane = lax.broadcasted_iota(jnp.int32, (BLOCK, BLOCK), 1)

    iq = iq_ref[...].reshape(IDX_HEADS * BLOCK, IDX_DIM)
    wq = ikw_ref[pl.ds(pl.multiple_of(i * BLOCK, BLOCK), BLOCK), :]
    wcols = [wq[:, IDX_DIM + hh:IDX_DIM + hh + 1] * (IDX_DIM ** -0.5 * IDX_HEADS ** -0.5)
             for hh in range(IDX_HEADS)]

    def score_body(j, carry):
        mn, mx = carry
        ik = ikw_ref[pl.ds(pl.multiple_of(j * BLOCK, BLOCK), BLOCK), :][:, :IDX_DIM].astype(_BF)
        raw = lax.dot_general(iq, ik, (((1,), (1,)), ((), ())), preferred_element_type=_F32)
        s = jnp.zeros((BLOCK, BLOCK), _F32)
        for hh in range(IDX_HEADS):
            s = s + jnp.maximum(raw[hh * BLOCK:(hh + 1) * BLOCK], 0.0) * wcols[hh]
        s = jnp.where(s == 0.0, 0.0, s)
        kpos = j * BLOCK + lane
        adm = (kpos <= row_t) & (kpos >= PAD)
        score_ref[j] = jnp.where(adm, s, -jnp.inf)
        mn = jnp.minimum(mn, jnp.min(jnp.where(adm, s, jnp.inf), axis=-1, keepdims=True))
        mx = jnp.maximum(mx, jnp.max(jnp.where(adm, s, -jnp.inf), axis=-1, keepdims=True))
        return mn, mx

    mn0 = jnp.full((BLOCK, 1), jnp.inf, _F32)
    mx0 = jnp.full((BLOCK, 1), -jnp.inf, _F32)
    lo, hi = lax.fori_loop(0, nkb, score_body, (mn0, mx0))

    n_adm = (i * BLOCK + lax.broadcasted_iota(jnp.int32, (BLOCK, 1), 0) - (PAD - 1)).astype(_F32)
    take_all = n_adm <= kf

    def count_gt(thr):
        def body(j, acc):
            return acc + jnp.where(score_ref[j] > thr, 1.0, 0.0)
        acc = lax.fori_loop(0, nkb, body, jnp.zeros((BLOCK, BLOCK), _F32))
        return jnp.sum(acc, axis=-1, keepdims=True)

    def count_eq(thr):
        def body(j, acc):
            return acc + jnp.where(score_ref[j] == thr, 1.0, 0.0)
        acc = lax.fori_loop(0, nkb, body, jnp.zeros((BLOCK, BLOCK), _F32))
        return jnp.sum(acc, axis=-1, keepdims=True)

    def max_where(pred):
        def body(j, m):
            s = score_ref[j]
            return jnp.maximum(m, jnp.where(pred(s), s, -jnp.inf))
        m = lax.fori_loop(0, nkb, body, jnp.full((BLOCK, BLOCK), -jnp.inf, _F32))
        return jnp.max(m, axis=-1, keepdims=True)

    def select_threshold():
        def bis(_, st):
            lo, hi, c_hi = st
            mid = (lo + hi) * 0.5
            c = count_gt(mid)
            take = c < kf
            return (jnp.where(take, lo, mid), jnp.where(take, mid, hi), jnp.where(take, c, c_hi))

        lo_f, hi_f, c_gt = lax.fori_loop(0, BISECT_ITERS, bis, (lo, hi, jnp.zeros((BLOCK, 1), _F32)))
        thr = max_where(lambda s: s <= hi_f)
        g = count_eq(thr)

        def not_done(st):
            thr, c_gt, g, it = st
            pending = jnp.where(take_all, 0.0, jnp.where(c_gt + g >= kf, 0.0, 1.0))
            return jnp.logical_and(jnp.max(pending) > 0.0, it < 4096)

        def peel(st):
            thr, c_gt, g, it = st
            done = jnp.logical_or(take_all, c_gt + g >= kf)
            thr_n = max_where(lambda s: s < thr)
            g_n = count_eq(thr_n)
            return (jnp.where(done, thr, thr_n), jnp.where(done, c_gt, c_gt + g),
                    jnp.where(done, g, g_n), it + 1)

        thr, c_gt, g, _ = lax.while_loop(not_done, peel, (thr, c_gt, g, jnp.int32(0)))
        return thr, c_gt

    thr, c_gt = lax.cond(i * BLOCK + (BLOCK - 1) - (PAD - 1) > k_top, select_threshold,
                         lambda: (jnp.zeros((BLOCK, 1), _F32), jnp.zeros((BLOCK, 1), _F32)))
    thr = jnp.where(take_all, -jnp.inf, thr)
    need = kf - c_gt

    q4 = qlat_ref[...].reshape(A_HEADS * BLOCK, KV_RANK)

    def attn_body(j, st):
        m, l, acc, eq_before = st
        s = score_ref[j]
        gt = s > thr
        eq = (s == thr) & (s > -jnp.inf)
        pre = jnp.dot(jnp.where(eq, 1.0, 0.0).astype(_BF), tri_ref[...], preferred_element_type=_F32)
        rank = eq_before + pre[:, :BLOCK]
        sel = gt | (eq & (rank < need))
        ckv = ckv_ref[pl.ds(pl.multiple_of(j * BLOCK, BLOCK), BLOCK), :]
        logits = lax.dot_general(q4, ckv, (((1,), (1,)), ((), ())), preferred_element_type=_F32)
        logits = logits.reshape(A_HEADS, BLOCK, BLOCK) + bias_ref[jnp.minimum(i - j, 2)]
        logits = jnp.where(sel[None], logits, NEG_INF)
        m_new = jnp.maximum(m, jnp.max(logits, axis=-1, keepdims=True))
        alpha = jnp.exp(m - m_new)
        p = jnp.exp(logits - m_new)
        l = alpha * l + jnp.sum(p, axis=-1, keepdims=True)
        pv = jnp.dot(p.reshape(A_HEADS * BLOCK, BLOCK).astype(_BF), ckv, preferred_element_type=_F32)
        acc = alpha * acc + pv.reshape(A_HEADS, BLOCK, KV_RANK)
        return m_new, l, acc, eq_before + pre[:, BLOCK:]

    st0 = (jnp.full((A_HEADS, BLOCK, 1), NEG_INF, _F32), jnp.zeros((A_HEADS, BLOCK, 1), _F32),
           jnp.zeros((A_HEADS, BLOCK, KV_RANK), _F32), jnp.zeros((BLOCK, BLOCK), _F32))
    _, l, acc, _ = lax.fori_loop(0, nkb, attn_body, st0)
    o_lat = (acc / l).astype(_BF)
    out = jnp.zeros((BLOCK, A_WIDTH), _F32)
    for hh in range(A_HEADS):
        out = out + jnp.dot(o_lat[hh], wuv_ref[hh], preferred_element_type=_F32)
    ya_ref[...] = out.astype(_BF)


def _dsa(k_top, rel_bias, qlat4, iq4, ikw3, ckv3, bucket, wuv_pad, tri_ext):
    bsz, plen, _ = ckv3.shape
    nb = plen // BLOCK
    return pl.pallas_call(
        functools.partial(_dsa_kernel, k_top),
        grid=(bsz, nb),
        in_specs=[
            pl.BlockSpec(memory_space=pltpu.SMEM),
            pl.BlockSpec((A_HEADS, BLOCK, KV_RANK), lambda b, i: (0, b * nb + i, 0)),
            pl.BlockSpec((IDX_HEADS, BLOCK, IDX_DIM), lambda b, i: (0, b * nb + i, 0)),
            pl.BlockSpec((None, plen, 128), lambda b, i: (b, 0, 0)),
            pl.BlockSpec((None, plen, KV_RANK), lambda b, i: (b, 0, 0)),
            _const_spec((2, BLOCK, BLOCK)),
            _const_spec((A_HEADS, KV_RANK, A_WIDTH)),
            _const_spec((BLOCK, 2 * BLOCK)),
        ],
        out_specs=pl.BlockSpec((BLOCK, A_WIDTH), lambda b, i: (b * nb + i, 0)),
        out_shape=jax.ShapeDtypeStruct((bsz * plen, A_WIDTH), _BF),
        scratch_shapes=[pltpu.VMEM((nb, BLOCK, BLOCK), _F32),
                        pltpu.VMEM((3, A_HEADS, BLOCK, BLOCK), _F32)],
        compiler_params=pltpu.CompilerParams(dimension_semantics=("arbitrary", "arbitrary"),
                                             vmem_limit_bytes=VMEM_LIMIT),
        name="dsa",
    )(rel_bias, qlat4, iq4, ikw3, ckv3, bucket, wuv_pad, tri_ext)


def _stick_kernel(q_ref, k_ref, v_ref, low_ref, yc_ref):
    i = pl.program_id(1)
    row_t = i * BLOCK + lax.broadcasted_iota(jnp.int32, (BLOCK, BLOCK), 0)
    lane = lax.broadcasted_iota(jnp.int32, (BLOCK, BLOCK), 1)
    first_head = lane < HEAD_DIM

    outs = []
    for pr in range(C_HEADS // 2):
        cs = slice(pr * BLOCK, (pr + 1) * BLOCK)
        qp = q_ref[:, cs].astype(_F32)
        qm = [jnp.where(first_head, qp, 0.0).astype(_BF), jnp.where(first_head, 0.0, qp).astype(_BF)]

        def body(step, st, cs=cs, qm=qm):
            carry0, carry1, acc = st
            j = i - step
            r0 = pl.multiple_of(j * BLOCK, BLOCK)
            kp = k_ref[pl.ds(r0, BLOCK), cs]
            vp = v_ref[pl.ds(r0, BLOCK), cs].astype(_F32)
            kpos = j * BLOCK + lane
            strict = (kpos < row_t) & (kpos >= PAD)
            ws, carries = [], []
            for hh, carry in enumerate((carry0, carry1)):
                z = lax.dot_general(qm[hh], kp, (((1,), (1,)), ((), ())), preferred_element_type=_F32)
                lk = -(jnp.maximum(z, 0.0) + jnp.log(1.0 + jnp.exp(-jnp.abs(z))))
                lk = jnp.where(strict, lk, 0.0)
                lk_hi = lk.astype(_BF)
                lk_lo = (lk - lk_hi.astype(_F32)).astype(_BF)
                cum = jnp.dot(jnp.concatenate([lk_hi, lk_lo], axis=1), low_ref[...],
                              preferred_element_type=_F32)
                w = jnp.where(strict, jnp.exp(z + lk + cum[:, :BLOCK] + carry), 0.0)
                ws.append(w.astype(_BF))
                carries.append(carry + cum[:, BLOCK:])
            vm = jnp.concatenate([jnp.where(first_head, vp, 0.0).astype(_BF),
                                  jnp.where(first_head, 0.0, vp).astype(_BF)], axis=0)
            acc = acc + jnp.dot(jnp.concatenate(ws, axis=1), vm, preferred_element_type=_F32)
            return carries[0], carries[1], acc

        z0 = jnp.zeros((BLOCK, BLOCK), _F32)
        _, _, acc = lax.fori_loop(0, i + 1, body, (z0, z0, z0))
        outs.append(acc)
    yc_ref[...] = jnp.concatenate(outs, axis=1).astype(_BF)


def _stick(qc2d, kc3, vc3, low_ext):
    bsz, plen, _ = kc3.shape
    nb = plen // BLOCK
    return pl.pallas_call(
        _stick_kernel,
        grid=(bsz, nb),
        in_specs=[
            pl.BlockSpec((BLOCK, C_WIDTH), lambda b, i: (b * nb + i, 0)),
            pl.BlockSpec((None, plen, C_WIDTH), lambda b, i: (b, 0, 0)),
            pl.BlockSpec((None, plen, C_WIDTH), lambda b, i: (b, 0, 0)),
            _const_spec((2 * BLOCK, 2 * BLOCK)),
        ],
        out_specs=pl.BlockSpec((BLOCK, C_WIDTH), lambda b, i: (b * nb + i, 0)),
        out_shape=jax.ShapeDtypeStruct((bsz * plen, C_WIDTH), _BF),
        compiler_params=pltpu.CompilerParams(dimension_semantics=("arbitrary", "arbitrary"),
                                             vmem_limit_bytes=VMEM_LIMIT),
        name="stick",
    )(qc2d, kc3, vc3, low_ext)


def _s5_kernel(bsz, u_ref, wbu_ref, a_ref, wc_ref, d_ref, wglu_ref, y_ref, bur_ref, bui_ref, st_ref):
    c = pl.program_id(0)
    slab_w = S5_LANES // S5_SLABS
    ch_w = B_WIDTH // S5_SLABS

    @pl.when(c == 0)
    def _():
        st_ref[...] = jnp.zeros_like(st_ref)

    u = u_ref[...]
    for k in range(S5_SLABS):
        o = jnp.dot(u[:, k * ch_w:(k + 1) * ch_w], wbu_ref[k], preferred_element_type=_F32)
        bur_ref[:, k * slab_w:(k + 1) * slab_w] = o[:, :slab_w]
        bui_ref[:, k * slab_w:(k + 1) * slab_w] = o[:, slab_w:]

    for k in range(S5_SLABS):
        ls = slice(k * slab_w, (k + 1) * slab_w)
        ar = jnp.broadcast_to(a_ref[0:1, ls], (bsz, slab_w))
        ai = jnp.broadcast_to(a_ref[1:2, ls], (bsz, slab_w))

        def step(t, st, ls=ls, ar=ar, ai=ai):
            hr, hi = st
            rows = pl.ds(pl.multiple_of(t * bsz, bsz), bsz)
            nhr = ar * hr - ai * hi + bur_ref[rows, ls]
            nhi = ar * hi + ai * hr + bui_ref[rows, ls]
            bur_ref[rows, ls] = nhr
            bui_ref[rows, ls] = nhi
            return nhr, nhi

        hr, hi = lax.fori_loop(0, S5_CHUNK, step, (st_ref[0, :, ls], st_ref[1, :, ls]), unroll=4)
        st_ref[0, :, ls] = hr
        st_ref[1, :, ls] = hi

    ys = []
    for k in range(S5_SLABS):
        ls = slice(k * slab_w, (k + 1) * slab_w)
        hcat = jnp.concatenate([bur_ref[:, ls].astype(_BF), bui_ref[:, ls].astype(_BF)], axis=1)
        ys.append(jnp.dot(hcat, wc_ref[k], preferred_element_type=_F32))
    y = jnp.concatenate(ys, axis=1) + d_ref[...] * u.astype(_F32)
    y = 0.5 * y * (1.0 + jnp.tanh(math.sqrt(2.0 / math.pi) * (y + 0.044715 * (y * y * y))))
    gate = jnp.dot(y.astype(_BF), wglu_ref[...], preferred_element_type=_F32)
    y_ref[...] = (y * (1.0 / (1.0 + jnp.exp(-gate)))).astype(_BF)


def _s5(bsz, u_tb, wbu, a_pack, wc, d_skip, w_glu):
    rows = u_tb.shape[0]
    rc = S5_CHUNK * bsz
    return pl.pallas_call(
        functools.partial(_s5_kernel, bsz),
        grid=(rows // rc,),
        in_specs=[
            pl.BlockSpec((rc, B_WIDTH), lambda c: (c, 0)),
            _const_spec((S5_SLABS, B_WIDTH // S5_SLABS, 2 * S5_LANES // S5_SLABS)),
            _const_spec((2, S5_LANES)),
            _const_spec((S5_SLABS, 2 * S5_LANES // S5_SLABS, B_WIDTH // S5_SLABS)),
            _const_spec((1, B_WIDTH)),
            _const_spec((B_WIDTH, B_WIDTH)),
        ],
        out_specs=pl.BlockSpec((rc, B_WIDTH), lambda c: (c, 0)),
        out_shape=jax.ShapeDtypeStruct((rows, B_WIDTH), _BF),
        scratch_shapes=[pltpu.VMEM((rc, S5_LANES), _F32), pltpu.VMEM((rc, S5_LANES), _F32),
                        pltpu.VMEM((2, bsz, S5_LANES), _F32)],
        compiler_params=pltpu.CompilerParams(dimension_semantics=("arbitrary",),
                                             vmem_limit_bytes=VMEM_LIMIT),
        name="s5",
    )(u_tb, wbu, a_pack, wc, d_skip, w_glu)


def _mix_mlp_kernel(plen, final, h_ref, ya_ref, yb_ref, yc_ref, ga_ref, gb_ref, gc_ref, wo_ref,
                    g2_ref, w1_ref, w2_ref, gf_ref, o_ref):
    tm = h_ref.shape[0]
    y = jnp.concatenate([_rms(ya_ref[...].astype(_F32), ga_ref[...]),
                         _rms(yb_ref[...].astype(_F32), gb_ref[...]),
                         _rms(yc_ref[...].astype(_F32), gc_ref[...])], axis=1).astype(_BF)
    h = h_ref[...] + jnp.dot(y, wo_ref[...], preferred_element_type=_F32)
    hn = _rms(h, g2_ref[...]).astype(_BF)
    ffc = D_FF // 4
    for cc in range(4):
        a = jnp.dot(hn, w1_ref[:, cc * ffc:(cc + 1) * ffc], preferred_element_type=_F32)
        a = jnp.square(jnp.maximum(a, 0.0)).astype(_BF)
        h = h + jnp.dot(a, w2_ref[cc * ffc:(cc + 1) * ffc, :], preferred_element_type=_F32)
    if final:
        h = _rms(h, gf_ref[...])
    r = (pl.program_id(0) * tm + lax.broadcasted_iota(jnp.int32, (tm, 1), 0)).astype(_F32)
    pos = r - jnp.floor((r + 0.5) * (1.0 / plen)) * plen
    o_ref[...] = jnp.where(pos >= PAD, h, 0.0)


def _mix_mlp(plen, final, h2d, ya, yb, yc, ga, gb, gc, wo, g2, w1, w2, gf):
    rows = h2d.shape[0]
    tm = ROW_TILE
    row = lambda w: pl.BlockSpec((tm, w), lambda i: (i, 0))
    return pl.pallas_call(
        functools.partial(_mix_mlp_kernel, plen, final),
        grid=(rows // tm,),
        in_specs=[row(D_MODEL), row(A_WIDTH), row(B_WIDTH), row(C_WIDTH),
                  _const_spec((1, A_WIDTH)), _const_spec((1, B_WIDTH)), _const_spec((1, C_WIDTH)),
                  _const_spec((D_MODEL, D_MODEL)), _const_spec((1, D_MODEL)),
                  _const_spec((D_MODEL, D_FF)), _const_spec((D_FF, D_MODEL)), _const_spec((1, D_MODEL))],
        out_specs=row(D_MODEL),
        out_shape=jax.ShapeDtypeStruct((rows, D_MODEL), _F32),
        compiler_params=pltpu.CompilerParams(dimension_semantics=("arbitrary",),
                                             vmem_limit_bytes=VMEM_LIMIT),
        name="mix_mlp",
    )(h2d, ya, yb, yc, ga, gb, gc, wo, g2, w1, w2, gf)


def _t5_bucket_tiles():
    max_exact = NUM_BUCKETS // 2
    r = jnp.arange(BLOCK)[:, None]
    c = jnp.arange(BLOCK)[None, :]
    tiles = []
    for dd in range(2):
        n = jnp.maximum(dd * BLOCK + r - c, 0)
        nf = jnp.maximum(n, max_exact).astype(_F32)
        large = max_exact + (jnp.log(nf / max_exact) / math.log(MAX_DISTANCE / max_exact)
                             * (NUM_BUCKETS - max_exact)).astype(jnp.int32)
        large = jnp.minimum(large, NUM_BUCKETS - 1)
        tiles.append(jnp.where(n < max_exact, n, large))
    return jnp.stack(tiles).astype(jnp.int32)


def _pack_w_in(w_in):
    sizes = (A_WIDTH, KV_RANK, IDX_HEADS * IDX_DIM, IDX_DIM, IDX_HEADS, B_WIDTH, C_WIDTH, C_WIDTH, C_WIDTH)
    offs = np.cumsum((0,) + sizes)
    qa, ckv, iq, ik, iw, ub, qc, kc, vc = [w_in[:, offs[n]:offs[n + 1]] for n in range(len(sizes))]
    zeros = lambda n: jnp.zeros((w_in.shape[0], n), w_in.dtype)
    cols = [qa, ckv]
    for hh in range(IDX_HEADS):
        cols += [iq[:, hh * IDX_DIM:(hh + 1) * IDX_DIM], zeros(128 - IDX_DIM)]
    cols += [ik, iw, zeros(128 - IDX_DIM - IDX_HEADS), ub, qc, kc, vc]
    return jnp.concatenate(cols, axis=1).astype(_BF)


def _block_diag(blocks):
    n, r, c = blocks.shape
    eye = jnp.eye(n, dtype=blocks.dtype)
    return (eye[:, None, :, None] * blocks[:, :, None, :]).reshape(n * r, n * c)


def _s5_params(lam_re, lam_im, log_dt, b_re, b_im, c_re, c_im):
    dt = jnp.exp(log_dt)[:, None]
    mag = jnp.exp(lam_re * dt)
    ar = mag * jnp.cos(lam_im * dt)
    ai = mag * jnp.sin(lam_im * dt)
    den = lam_re * lam_re + lam_im * lam_im
    nr, ni = ar - 1.0, ai
    fr = (nr * lam_re + ni * lam_im) / den
    fi = (ni * lam_re - nr * lam_im) / den
    bfr = fr[:, :, None] * b_re - fi[:, :, None] * b_im
    bfi = fr[:, :, None] * b_im + fi[:, :, None] * b_re
    gps = S5_GROUPS // S5_SLABS
    wbu, wc = [], []
    for k in range(S5_SLABS):
        gs = slice(k * gps, (k + 1) * gps)
        wbu.append(jnp.concatenate([_block_diag(jnp.swapaxes(bfr[gs], 1, 2)),
                                    _block_diag(jnp.swapaxes(bfi[gs], 1, 2))], axis=1))
        wc.append(jnp.concatenate([_block_diag(jnp.swapaxes(c_re[gs], 1, 2)),
                                   _block_diag(jnp.swapaxes(-c_im[gs], 1, 2))], axis=0))
    a_pack = jnp.stack([ar.reshape(-1), ai.reshape(-1)])
    return jnp.stack(wbu).astype(_BF), a_pack, jnp.stack(wc).astype(_BF)


def kernel(x, meta_tokens, rel_bias, norm1_g, w_in, kv_norm_g, w_uk, w_uv, lambda_re, lambda_im, log_dt,
           b_re, b_im, c_re, c_im, d_skip, w_glu, gn_a, gn_b, gn_c, w_out, norm2_g, w_mlp1, w_mlp2, final_g):
    bsz, seq, _ = x.shape
    depth = w_in.shape[0]
    n_keys = seq + N_META
    k_top = min(TOPK_MAX, n_keys // 4)
    plen = n_keys + PAD
    rows = bsz * plen
    assert seq % BLOCK == 0 and rows % ROW_TILE == 0 and plen % S5_CHUNK == 0 and bsz % 8 == 0

    head = jnp.concatenate([jnp.zeros((PAD, D_MODEL), x.dtype), meta_tokens.astype(x.dtype)], axis=0)
    h = jnp.concatenate([jnp.broadcast_to(head[None], (bsz, BLOCK, D_MODEL)), x], axis=1).reshape(rows, D_MODEL)

    bucket = _t5_bucket_tiles()
    ar_ = jnp.arange(BLOCK)
    upper = (ar_[:, None] < ar_[None, :]).astype(_BF)
    lower = (ar_[:, None] > ar_[None, :]).astype(_BF)
    ones = jnp.ones((BLOCK, BLOCK), _BF)
    tri_ext = jnp.concatenate([upper, ones], axis=1)
    low_ext = jnp.tile(jnp.concatenate([lower, ones], axis=1), (2, 1))
    row1 = lambda v: v.reshape(1, -1).astype(_F32)

    for l in range(depth):
        w_all = _pack_w_in(w_in[l])
        wuk_bd = _block_diag(jnp.transpose(w_uk[l], (1, 2, 0))).astype(_BF)
        wuv_pad = jnp.stack([jnp.pad(w_uv[l][:, hh, :], ((0, 0), (hh * HEAD_DIM, A_WIDTH - (hh + 1) * HEAD_DIM)))
                             for hh in range(A_HEADS)]).astype(_BF)
        qlat4, ckv, iq4, ikw, ub, qc, kc, vc = _inproj(h, row1(norm1_g[l]), w_all, row1(kv_norm_g[l]), wuk_bd)

        ya = _dsa(k_top, rel_bias.astype(_F32), qlat4, iq4, ikw.reshape(bsz, plen, 128),
                  ckv.reshape(bsz, plen, KV_RANK), bucket, wuv_pad, tri_ext)

        wbu, a_pack, wc = _s5_params(lambda_re[l], lambda_im[l], log_dt[l], b_re[l], b_im[l], c_re[l], c_im[l])
        u_tb = jnp.swapaxes(ub.reshape(bsz, plen, B_WIDTH), 0, 1).reshape(rows, B_WIDTH)
        yb_tb = _s5(bsz, u_tb, wbu, a_pack, wc, row1(d_skip[l]), w_glu[l].astype(_BF))
        yb = jnp.swapaxes(yb_tb.reshape(plen, bsz, B_WIDTH), 0, 1).reshape(rows, B_WIDTH)

        yc = _stick(qc, kc.reshape(bsz, plen, C_WIDTH), vc.reshape(bsz, plen, C_WIDTH), low_ext)

        h = _mix_mlp(plen, l == depth - 1, h, ya, yb, yc, row1(gn_a[l]), row1(gn_b[l]), row1(gn_c[l]),
                     w_out[l].astype(_BF), row1(norm2_g[l]), w_mlp1[l].astype(_BF), w_mlp2[l].astype(_BF),
                     row1(final_g))

    return h.reshape(bsz, plen, D_MODEL)[:, BLOCK:]
```

```python
import functools
import math

import jax
import jax.numpy as jnp
import numpy as np
from jax import lax
from jax.experimental import pallas as pl
from jax.experimental.pallas import tpu as pltpu

D_MODEL = 1024
N_META = 16
BLOCK = 128
PAD = BLOCK - N_META
HEAD_DIM = 64
A_WIDTH = 256
A_HEADS = 4
KV_RANK = 128
IDX_HEADS = 4
IDX_DIM = 64
TOPK_MAX = 256
B_WIDTH = 512
S5_GROUP = 16
S5_GROUPS = 32
S5_STATE = 64
C_WIDTH = 256
C_HEADS = 4
D_FF = 4 * D_MODEL
NUM_BUCKETS = 32
MAX_DISTANCE = 128
RMS_EPS = 1e-6
NEG_INF = -1e30

ROW_TILE = 512
S5_CHUNK = 32
S5_LANES = S5_GROUPS * S5_STATE
S5_SLABS = 4
BISECT_ITERS = 17
VMEM_LIMIT = 56 * 1024 * 1024

_BF = jnp.bfloat16
_F32 = jnp.float32

_C_QA = 0
_C_CKV = _C_QA + A_WIDTH
_C_IQ = _C_CKV + KV_RANK
_C_IKW = _C_IQ + IDX_HEADS * 128
_C_UB = _C_IKW + 128
_C_QC = _C_UB + B_WIDTH
_C_KC = _C_QC + C_WIDTH
_C_VC = _C_KC + C_WIDTH
_N_PROJ = _C_VC + C_WIDTH


def _rms(x, g):
    return x * lax.rsqrt(jnp.mean(x * x, axis=-1, keepdims=True) + RMS_EPS) * g


def _inproj_kernel(h_ref, g1_ref, w_ref, kvg_ref, wuk_ref,
                   qlat_ref, ckv_ref, iq_ref, ikw_ref, ub_ref, qc_ref, kc_ref, vc_ref):
    hn = _rms(h_ref[...], g1_ref[...]).astype(_BF)
    proj = jnp.dot(hn, w_ref[...], preferred_element_type=_F32)
    qa = proj[:, _C_QA:_C_QA + A_WIDTH].astype(_BF)
    qlat = jnp.dot(qa, wuk_ref[...], preferred_element_type=_F32) * (HEAD_DIM ** -0.5)
    for hh in range(A_HEADS):
        qlat_ref[hh] = qlat[:, hh * KV_RANK:(hh + 1) * KV_RANK].astype(_BF)
    ckv_ref[...] = _rms(proj[:, _C_CKV:_C_CKV + KV_RANK], kvg_ref[...]).astype(_BF)
    for hh in range(IDX_HEADS):
        c0 = _C_IQ + hh * 128
        iq_ref[hh] = proj[:, c0:c0 + IDX_DIM].astype(_BF)
    ikw_ref[...] = proj[:, _C_IKW:_C_IKW + 128]
    ub_ref[...] = proj[:, _C_UB:_C_UB + B_WIDTH].astype(_BF)
    qc_ref[...] = (proj[:, _C_QC:_C_QC + C_WIDTH] * (HEAD_DIM ** -0.5)).astype(_BF)
    kc_ref[...] = proj[:, _C_KC:_C_KC + C_WIDTH].astype(_BF)
    vc_ref[...] = proj[:, _C_VC:_C_VC + C_WIDTH].astype(_BF)


def _const_spec(shape):
    nd = len(shape)
    return pl.BlockSpec(shape, lambda *_: (0,) * nd, pipeline_mode=pl.Buffered(1))


def _inproj(h2d, g1, w_all, kvg, wuk_bd):
    rows = h2d.shape[0]
    tm = ROW_TILE
    row = lambda w: pl.BlockSpec((tm, w), lambda i: (i, 0))
    head = lambda w: pl.BlockSpec((4, tm, w), lambda i: (0, i, 0))
    out_shape = (
        jax.ShapeDtypeStruct((A_HEADS, rows, KV_RANK), _BF),
        jax.ShapeDtypeStruct((rows, KV_RANK), _BF),
        jax.ShapeDtypeStruct((IDX_HEADS, rows, IDX_DIM), _BF),
        jax.ShapeDtypeStruct((rows, 128), _F32),
        jax.ShapeDtypeStruct((rows, B_WIDTH), _BF),
        jax.ShapeDtypeStruct((rows, C_WIDTH), _BF),
        jax.ShapeDtypeStruct((rows, C_WIDTH), _BF),
        jax.ShapeDtypeStruct((rows, C_WIDTH), _BF),
    )
    return pl.pallas_call(
        _inproj_kernel,
        grid=(rows // tm,),
        in_specs=[row(D_MODEL), _const_spec((1, D_MODEL)), _const_spec((D_MODEL, _N_PROJ)),
                  _const_spec((1, KV_RANK)), _const_spec((A_WIDTH, A_HEADS * KV_RANK))],
        out_specs=(head(KV_RANK), row(KV_RANK), head(IDX_DIM), row(128), row(B_WIDTH),
                   row(C_WIDTH), row(C_WIDTH), row(C_WIDTH)),
        out_shape=out_shape,
        compiler_params=pltpu.CompilerParams(dimension_semantics=("arbitrary",),
                                             vmem_limit_bytes=VMEM_LIMIT),
        name="inproj",
    )(h2d, g1, w_all, kvg, wuk_bd)


def _dsa_kernel(k_top, nb, relb_ref, qlat_ref, iq_ref, ikw_ref, ckv_ref, ckvt_ref, bucket_ref, wuvt_ref,
                early_ref, ya_ref, score_ref, logit_ref, acc_ref, bias_ref):
    i = pl.program_id(1)
    npair = (i + 2) // 2
    kf = float(k_top)
    hw = A_HEADS * BLOCK

    @pl.when(i == 0)
    def _():
        for dd in range(2):
            bk = bucket_ref[dd]
            for hh in range(A_HEADS):
                acc = jnp.zeros((BLOCK, BLOCK), _F32)
                for b_ in range(NUM_BUCKETS):
                    acc = jnp.where(bk == b_, relb_ref[b_, hh], acc)
                bias_ref[dd, :, hh * BLOCK:(hh + 1) * BLOCK] = acc
        for hh in range(A_HEADS):
            bias_ref[2, :, hh * BLOCK:(hh + 1) * BLOCK] = jnp.full((BLOCK, BLOCK), relb_ref[NUM_BUCKETS - 1, hh], _F32)

    key_s = lax.broadcasted_iota(jnp.int32, (BLOCK, BLOCK), 0)
    q_t = i * BLOCK + lax.broadcasted_iota(jnp.int32, (BLOCK, BLOCK), 1)

    def rows_of(j):
        return pl.ds(pl.multiple_of(jnp.minimum(j, nb - 1) * BLOCK, BLOCK), BLOCK)

    def col_sum(x):
        return jnp.sum(x, axis=0, keepdims=True)

    def fold8(x, op):
        return op(x.reshape(BLOCK // 8, 8, BLOCK), axis=0)

    iq = iq_ref[...].reshape(hw, IDX_DIM)
    wq_t = ikw_ref[pl.ds(pl.multiple_of(i * BLOCK, BLOCK), BLOCK), :].T
    wrow = [wq_t[IDX_DIM + hh:IDX_DIM + hh + 1, :] * (IDX_DIM ** -0.5 * IDX_HEADS ** -0.5)
            for hh in range(IDX_HEADS)]

    def score_block(j):
        ik = ikw_ref[rows_of(j), :][:, :IDX_DIM].astype(_BF)
        raw = lax.dot_general(ik, iq, (((1,), (1,)), ((), ())), preferred_element_type=_F32)
        s = jnp.zeros((BLOCK, BLOCK), _F32)
        for hh in range(IDX_HEADS):
            s = s + jnp.maximum(raw[:, hh * BLOCK:(hh + 1) * BLOCK], 0.0) * wrow[hh]
        s = jnp.where(s == 0.0, 0.0, s)
        kpos = j * BLOCK + key_s
        adm = (kpos <= q_t) & (kpos >= PAD)
        s_lo = jnp.where(adm, s, -jnp.inf)
        score_ref[j] = s_lo
        return jnp.where(adm, s, jnp.inf), s_lo

    def score_body(t, carry):
        mn, mx = carry
        a_hi, a_lo = score_block(2 * t)
        b_hi, b_lo = score_block(2 * t + 1)
        return (jnp.minimum(mn, fold8(jnp.minimum(a_hi, b_hi), jnp.min)),
                jnp.maximum(mx, fold8(jnp.maximum(a_lo, b_lo), jnp.max)))

    mn, mx = lax.fori_loop(0, npair, score_body, (jnp.full((8, BLOCK), jnp.inf, _F32),
                                                  jnp.full((8, BLOCK), -jnp.inf, _F32)))
    lo = jnp.min(mn, axis=0, keepdims=True)
    hi = jnp.max(mx, axis=0, keepdims=True)

    n_adm = (i * BLOCK + lax.broadcasted_iota(jnp.int32, (1, BLOCK), 1) - (PAD - 1)).astype(_F32)
    take_all = n_adm <= kf

    def count_where(pred):
        def body(t, acc):
            return (acc + jnp.where(pred(score_ref[2 * t]), 1.0, 0.0)
                    + jnp.where(pred(score_ref[2 * t + 1]), 1.0, 0.0))
        return col_sum(lax.fori_loop(0, npair, body, jnp.zeros((BLOCK, BLOCK), _F32)))

    def max_where(pred):
        def body(t, m):
            sa, sb = score_ref[2 * t], score_ref[2 * t + 1]
            return jnp.maximum(m, jnp.maximum(jnp.where(pred(sa), sa, -jnp.inf),
                                              jnp.where(pred(sb), sb, -jnp.inf)))
        m = lax.fori_loop(0, npair, body, jnp.full((BLOCK, BLOCK), -jnp.inf, _F32))
        return jnp.max(m, axis=0, keepdims=True)

    def select_threshold():
        def bis(_, st):
            lo, hi, c_hi = st
            mid = (lo + hi) * 0.5
            c = count_where(lambda s: s > mid)
            take = c < kf
            return (jnp.where(take, lo, mid), jnp.where(take, mid, hi), jnp.where(take, c, c_hi))

        _, hi_f, c_gt = lax.fori_loop(0, BISECT_ITERS, bis, (lo, hi, jnp.zeros((1, BLOCK), _F32)))
        thr = max_where(lambda s: s <= hi_f)
        g = count_where(lambda s: s == thr)

        def not_done(st):
            thr, c_gt, g, it = st
            pending = jnp.where(take_all, 0.0, jnp.where(c_gt + g >= kf, 0.0, 1.0))
            return jnp.logical_and(jnp.max(pending) > 0.0, it < 4096)

        def peel(st):
            thr, c_gt, g, it = st
            done = jnp.logical_or(take_all, c_gt + g >= kf)
            thr_n = max_where(lambda s: s < thr)
            g_n = count_where(lambda s: s == thr_n)
            return (jnp.where(done, thr, thr_n), jnp.where(done, c_gt, c_gt + g),
                    jnp.where(done, g, g_n), it + 1)

        thr, c_gt, g, _ = lax.while_loop(not_done, peel, (thr, c_gt, g, jnp.int32(0)))
        return thr, c_gt

    thr, c_gt = lax.cond(i * BLOCK + (BLOCK - 1) - (PAD - 1) > k_top, select_threshold,
                         lambda: (jnp.zeros((1, BLOCK), _F32), jnp.zeros((1, BLOCK), _F32)))
    thr = jnp.where(take_all, -jnp.inf, thr)
    need = kf - c_gt

    q4 = qlat_ref[...].reshape(hw, KV_RANK)

    def logit_block(j, eq_before, m8):
        s = score_ref[j]
        eq = (s == thr) & (s > -jnp.inf)
        eqf = jnp.where(eq, 1.0, 0.0)
        rank = eq_before + jnp.dot(early_ref[...], eqf.astype(_BF), preferred_element_type=_F32)
        sel = (s > thr) | (eq & (rank < need))
        lt = lax.dot_general(ckv_ref[rows_of(j), :], q4, (((1,), (1,)), ((), ())),
                             preferred_element_type=_F32)
        dd = jnp.clip(i - j, 0, 2)
        m_new = []
        for hh in range(A_HEADS):
            hs = slice(hh * BLOCK, (hh + 1) * BLOCK)
            blk = jnp.where(sel, lt[:, hs] + bias_ref[dd, :, hs], NEG_INF)
            logit_ref[j, :, hs] = blk
            m_new.append(jnp.maximum(m8[hh], fold8(blk, jnp.max)))
        return eq_before + col_sum(eqf), m_new

    def logit_body(t, st):
        eq_before, m8 = st
        eq_before, m8 = logit_block(2 * t, eq_before, m8)
        eq_before, m8 = logit_block(2 * t + 1, eq_before, m8)
        return eq_before, m8

    _, m8 = lax.fori_loop(0, npair, logit_body,
                          (jnp.zeros((1, BLOCK), _F32), [jnp.full((8, BLOCK), NEG_INF, _F32)] * A_HEADS))
    m = jnp.concatenate([jnp.max(x, axis=0, keepdims=True) for x in m8], axis=1)

    acc_ref[...] = jnp.zeros_like(acc_ref)

    def pv_block(j, l8):
        p = jnp.exp(logit_ref[j] - m)
        cols = pl.ds(pl.multiple_of(jnp.minimum(j, nb - 1) * BLOCK, BLOCK), BLOCK)
        acc_ref[...] += jnp.dot(ckvt_ref[:, cols], p.astype(_BF), preferred_element_type=_F32)
        return [l8[hh] + fold8(p[:, hh * BLOCK:(hh + 1) * BLOCK], jnp.sum) for hh in range(A_HEADS)]

    def pv_body(t, l8):
        return pv_block(2 * t + 1, pv_block(2 * t, l8))

    l8 = lax.fori_loop(0, npair, pv_body, [jnp.zeros((8, BLOCK), _F32)] * A_HEADS)
    l = jnp.concatenate([col_sum(x) for x in l8], axis=1)
    o_lat = (acc_ref[...] / l).astype(_BF)
    out_t = jnp.zeros((A_WIDTH, BLOCK), _F32)
    for hh in range(A_HEADS):
        out_t = out_t + jnp.dot(wuvt_ref[hh], o_lat[:, hh * BLOCK:(hh + 1) * BLOCK], preferred_element_type=_F32)
    ya_ref[...] = out_t.T.astype(_BF)


def _dsa(k_top, rel_bias, qlat4, iq4, ikw3, ckv3, ckvt3, bucket_t, wuvt_pad, early):
    bsz, plen, _ = ckv3.shape
    nb = plen // BLOCK
    hw = A_HEADS * BLOCK
    return pl.pallas_call(
        functools.partial(_dsa_kernel, k_top, nb),
        grid=(bsz, nb),
        in_specs=[
            pl.BlockSpec(memory_space=pltpu.SMEM),
            pl.BlockSpec((A_HEADS, BLOCK, KV_RANK), lambda b, i: (0, b * nb + i, 0)),
            pl.BlockSpec((IDX_HEADS, BLOCK, IDX_DIM), lambda b, i: (0, b * nb + i, 0)),
            pl.BlockSpec((None, plen, 128), lambda b, i: (b, 0, 0)),
            pl.BlockSpec((None, plen, KV_RANK), lambda b, i: (b, 0, 0)),
            pl.BlockSpec((None, KV_RANK, plen), lambda b, i: (b, 0, 0)),
            _const_spec((2, BLOCK, BLOCK)),
            _const_spec((A_HEADS, A_WIDTH, KV_RANK)),
            _const_spec((BLOCK, BLOCK)),
        ],
        out_specs=pl.BlockSpec((BLOCK, A_WIDTH), lambda b, i: (b * nb + i, 0)),
        out_shape=jax.ShapeDtypeStruct((bsz * plen, A_WIDTH), _BF),
        scratch_shapes=[pltpu.VMEM((nb + 1, BLOCK, BLOCK), _F32),
                        pltpu.VMEM((nb + 1, BLOCK, hw), _F32),
                        pltpu.VMEM((KV_RANK, hw), _F32),
                        pltpu.VMEM((3, BLOCK, hw), _F32)],
        compiler_params=pltpu.CompilerParams(dimension_semantics=("arbitrary", "arbitrary"),
                                             vmem_limit_bytes=VMEM_LIMIT),
        name="dsa",
    )(rel_bias, qlat4, iq4, ikw3, ckv3, ckvt3, bucket_t, wuvt_pad, early)


def _stick_kernel(nb, q_ref, k_ref, v_ref, low_ref, yc_ref, carry_ref, acc_ref):
    i = pl.program_id(1)
    npair = (i + 2) // 2
    row_t = i * BLOCK + lax.broadcasted_iota(jnp.int32, (BLOCK, BLOCK), 0)
    lane = lax.broadcasted_iota(jnp.int32, (BLOCK, BLOCK), 1)
    first_head = lane < HEAD_DIM

    qm = []
    for hh in range(C_HEADS):
        qp = q_ref[:, (hh // 2) * BLOCK:(hh // 2 + 1) * BLOCK].astype(_F32)
        keep = first_head if hh % 2 == 0 else jnp.logical_not(first_head)
        qm.append(jnp.where(keep, qp, 0.0).astype(_BF))

    carry_ref[...] = jnp.zeros_like(carry_ref)
    acc_ref[...] = jnp.zeros_like(acc_ref)

    def one_block(j):
        r0 = pl.multiple_of(jnp.minimum(j, nb - 1) * BLOCK, BLOCK)
        kpos = j * BLOCK + lane
        strict = (kpos < row_t) & (kpos >= PAD)
        for pr in range(C_HEADS // 2):
            cs = slice(pr * BLOCK, (pr + 1) * BLOCK)
            kp = k_ref[pl.ds(r0, BLOCK), cs]
            vp = v_ref[pl.ds(r0, BLOCK), cs].astype(_F32)
            ws = []
            for hh in (2 * pr, 2 * pr + 1):
                z = lax.dot_general(qm[hh], kp, (((1,), (1,)), ((), ())), preferred_element_type=_F32)
                lk = -(jnp.maximum(z, 0.0) + jnp.log(1.0 + jnp.exp(-jnp.abs(z))))
                lk = jnp.where(strict, lk, 0.0)
                lk_hi = lk.astype(_BF)
                lk_lo = (lk - lk_hi.astype(_F32)).astype(_BF)
                cum = jnp.dot(jnp.concatenate([lk_hi, lk_lo], axis=1), low_ref[...],
                              preferred_element_type=_F32)
                w = jnp.where(strict, jnp.exp(z + lk + cum[:, :BLOCK] + carry_ref[hh]), 0.0)
                ws.append(w.astype(_BF))
                carry_ref[hh] = carry_ref[hh] + cum[:, BLOCK:]
            vm = jnp.concatenate([jnp.where(first_head, vp, 0.0).astype(_BF),
                                  jnp.where(first_head, 0.0, vp).astype(_BF)], axis=0)
            acc_ref[:, cs] = acc_ref[:, cs] + jnp.dot(jnp.concatenate(ws, axis=1), vm,
                                                      preferred_element_type=_F32)

    def body(t, _):
        ja = 2 * (npair - 1 - t) + 1
        one_block(ja)
        one_block(ja - 1)
        return 0

    lax.fori_loop(0, npair, body, 0)
    yc_ref[...] = acc_ref[...].astype(_BF)


def _stick(qc2d, kc3, vc3, low_ext):
    bsz, plen, _ = kc3.shape
    nb = plen // BLOCK
    return pl.pallas_call(
        functools.partial(_stick_kernel, nb),
        grid=(bsz, nb),
        in_specs=[
            pl.BlockSpec((BLOCK, C_WIDTH), lambda b, i: (b * nb + i, 0)),
            pl.BlockSpec((None, plen, C_WIDTH), lambda b, i: (b, 0, 0)),
            pl.BlockSpec((None, plen, C_WIDTH), lambda b, i: (b, 0, 0)),
            _const_spec((2 * BLOCK, 2 * BLOCK)),
        ],
        out_specs=pl.BlockSpec((BLOCK, C_WIDTH), lambda b, i: (b * nb + i, 0)),
        out_shape=jax.ShapeDtypeStruct((bsz * plen, C_WIDTH), _BF),
        scratch_shapes=[pltpu.VMEM((C_HEADS, BLOCK, BLOCK), _F32),
                        pltpu.VMEM((BLOCK, C_WIDTH), _F32)],
        compiler_params=pltpu.CompilerParams(dimension_semantics=("arbitrary", "arbitrary"),
                                             vmem_limit_bytes=VMEM_LIMIT),
        name="stick",
    )(qc2d, kc3, vc3, low_ext)


def _s5_kernel(bsz, u_ref, wbu_ref, a_ref, wc_ref, d_ref, wglu_ref, y_ref, bur_ref, bui_ref, st_ref):
    c = pl.program_id(0)
    slab_w = S5_LANES // S5_SLABS
    ch_w = B_WIDTH // S5_SLABS

    @pl.when(c == 0)
    def _():
        st_ref[...] = jnp.zeros_like(st_ref)

    u = u_ref[...]
    for k in range(S5_SLABS):
        o = jnp.dot(u[:, k * ch_w:(k + 1) * ch_w], wbu_ref[k], preferred_element_type=_F32)
        bur_ref[:, k * slab_w:(k + 1) * slab_w] = o[:, :slab_w]
        bui_ref[:, k * slab_w:(k + 1) * slab_w] = o[:, slab_w:]

    for k in range(S5_SLABS):
        ls = slice(k * slab_w, (k + 1) * slab_w)
        ar = jnp.broadcast_to(a_ref[0:1, ls], (bsz, slab_w))
        ai = jnp.broadcast_to(a_ref[1:2, ls], (bsz, slab_w))

        def step(t, st, ls=ls, ar=ar, ai=ai):
            hr, hi = st
            rows = pl.ds(pl.multiple_of(t * bsz, bsz), bsz)
            nhr = ar * hr - ai * hi + bur_ref[rows, ls]
            nhi = ar * hi + ai * hr + bui_ref[rows, ls]
            bur_ref[rows, ls] = nhr
            bui_ref[rows, ls] = nhi
            return nhr, nhi

        hr, hi = lax.fori_loop(0, S5_CHUNK, step, (st_ref[0, :, ls], st_ref[1, :, ls]), unroll=4)
        st_ref[0, :, ls] = hr
        st_ref[1, :, ls] = hi

    ys = []
    for k in range(S5_SLABS):
        ls = slice(k * slab_w, (k + 1) * slab_w)
        hcat = jnp.concatenate([bur_ref[:, ls].astype(_BF), bui_ref[:, ls].astype(_BF)], axis=1)
        ys.append(jnp.dot(hcat, wc_ref[k], preferred_element_type=_F32))
    y = jnp.concatenate(ys, axis=1) + d_ref[...] * u.astype(_F32)
    y = 0.5 * y * (1.0 + jnp.tanh(math.sqrt(2.0 / math.pi) * (y + 0.044715 * (y * y * y))))
    gate = jnp.dot(y.astype(_BF), wglu_ref[...], preferred_element_type=_F32)
    y_ref[...] = (y * (1.0 / (1.0 + jnp.exp(-gate)))).astype(_BF)


def _s5(bsz, u_tb, wbu, a_pack, wc, d_skip, w_glu):
    rows = u_tb.shape[0]
    rc = S5_CHUNK * bsz
    return pl.pallas_call(
        functools.partial(_s5_kernel, bsz),
        grid=(rows // rc,),
        in_specs=[
            pl.BlockSpec((rc, B_WIDTH), lambda c: (c, 0)),
            _const_spec((S5_SLABS, B_WIDTH // S5_SLABS, 2 * S5_LANES // S5_SLABS)),
            _const_spec((2, S5_LANES)),
            _const_spec((S5_SLABS, 2 * S5_LANES // S5_SLABS, B_WIDTH // S5_SLABS)),
            _const_spec((1, B_WIDTH)),
            _const_spec((B_WIDTH, B_WIDTH)),
        ],
        out_specs=pl.BlockSpec((rc, B_WIDTH), lambda c: (c, 0)),
        out_shape=jax.ShapeDtypeStruct((rows, B_WIDTH), _BF),
        scratch_shapes=[pltpu.VMEM((rc, S5_LANES), _F32), pltpu.VMEM((rc, S5_LANES), _F32),
                        pltpu.VMEM((2, bsz, S5_LANES), _F32)],
        compiler_params=pltpu.CompilerParams(dimension_semantics=("arbitrary",),
                                             vmem_limit_bytes=VMEM_LIMIT),
        name="s5",
    )(u_tb, wbu, a_pack, wc, d_skip, w_glu)


def _mix_mlp_kernel(plen, final, h_ref, ya_ref, yb_ref, yc_ref, ga_ref, gb_ref, gc_ref, wo_ref,
                    g2_ref, w1_ref, w2_ref, gf_ref, o_ref):
    tm = h_ref.shape[0]
    y = jnp.concatenate([_rms(ya_ref[...].astype(_F32), ga_ref[...]),
                         _rms(yb_ref[...].astype(_F32), gb_ref[...]),
                         _rms(yc_ref[...].astype(_F32), gc_ref[...])], axis=1).astype(_BF)
    h = h_ref[...] + jnp.dot(y, wo_ref[...], preferred_element_type=_F32)
    hn = _rms(h, g2_ref[...]).astype(_BF)
    ffc = D_FF // 4
    for cc in range(4):
        a = jnp.dot(hn, w1_ref[:, cc * ffc:(cc + 1) * ffc], preferred_element_type=_F32)
        a = jnp.square(jnp.maximum(a, 0.0)).astype(_BF)
        h = h + jnp.dot(a, w2_ref[cc * ffc:(cc + 1) * ffc, :], preferred_element_type=_F32)
    if final:
        h = _rms(h, gf_ref[...])
    r = (pl.program_id(0) * tm + lax.broadcasted_iota(jnp.int32, (tm, 1), 0)).astype(_F32)
    pos = r - jnp.floor((r + 0.5) * (1.0 / plen)) * plen
    o_ref[...] = jnp.where(pos >= PAD, h, 0.0)


def _mix_mlp(plen, final, h2d, ya, yb, yc, ga, gb, gc, wo, g2, w1, w2, gf):
    rows = h2d.shape[0]
    tm = ROW_TILE
    row = lambda w: pl.BlockSpec((tm, w), lambda i: (i, 0))
    return pl.pallas_call(
        functools.partial(_mix_mlp_kernel, plen, final),
        grid=(rows // tm,),
        in_specs=[row(D_MODEL), row(A_WIDTH), row(B_WIDTH), row(C_WIDTH),
                  _const_spec((1, A_WIDTH)), _const_spec((1, B_WIDTH)), _const_spec((1, C_WIDTH)),
                  _const_spec((D_MODEL, D_MODEL)), _const_spec((1, D_MODEL)),
                  _const_spec((D_MODEL, D_FF)), _const_spec((D_FF, D_MODEL)), _const_spec((1, D_MODEL))],
        out_specs=row(D_MODEL),
        out_shape=jax.ShapeDtypeStruct((rows, D_MODEL), _F32),
        compiler_params=pltpu.CompilerParams(dimension_semantics=("arbitrary",),
                                             vmem_limit_bytes=VMEM_LIMIT),
        name="mix_mlp",
    )(h2d, ya, yb, yc, ga, gb, gc, wo, g2, w1, w2, gf)


def _t5_bucket_tiles():
    max_exact = NUM_BUCKETS // 2
    r = jnp.arange(BLOCK)[:, None]
    c = jnp.arange(BLOCK)[None, :]
    tiles = []
    for dd in range(2):
        n = jnp.maximum(dd * BLOCK + r - c, 0)
        nf = jnp.maximum(n, max_exact).astype(_F32)
        large = max_exact + (jnp.log(nf / max_exact) / math.log(MAX_DISTANCE / max_exact)
                             * (NUM_BUCKETS - max_exact)).astype(jnp.int32)
        large = jnp.minimum(large, NUM_BUCKETS - 1)
        tiles.append(jnp.where(n < max_exact, n, large))
    return jnp.stack(tiles).astype(jnp.int32)


def _pack_w_in(w_in):
    sizes = (A_WIDTH, KV_RANK, IDX_HEADS * IDX_DIM, IDX_DIM, IDX_HEADS, B_WIDTH, C_WIDTH, C_WIDTH, C_WIDTH)
    offs = np.cumsum((0,) + sizes)
    qa, ckv, iq, ik, iw, ub, qc, kc, vc = [w_in[:, offs[n]:offs[n + 1]] for n in range(len(sizes))]
    zeros = lambda n: jnp.zeros((w_in.shape[0], n), w_in.dtype)
    cols = [qa, ckv]
    for hh in range(IDX_HEADS):
        cols += [iq[:, hh * IDX_DIM:(hh + 1) * IDX_DIM], zeros(128 - IDX_DIM)]
    cols += [ik, iw, zeros(128 - IDX_DIM - IDX_HEADS), ub, qc, kc, vc]
    return jnp.concatenate(cols, axis=1).astype(_BF)


def _block_diag(blocks):
    n, r, c = blocks.shape
    eye = jnp.eye(n, dtype=blocks.dtype)
    return (eye[:, None, :, None] * blocks[:, :, None, :]).reshape(n * r, n * c)


def _s5_params(lam_re, lam_im, log_dt, b_re, b_im, c_re, c_im):
    dt = jnp.exp(log_dt)[:, None]
    mag = jnp.exp(lam_re * dt)
    ar = mag * jnp.cos(lam_im * dt)
    ai = mag * jnp.sin(lam_im * dt)
    den = lam_re * lam_re + lam_im * lam_im
    nr, ni = ar - 1.0, ai
    fr = (nr * lam_re + ni * lam_im) / den
    fi = (ni * lam_re - nr * lam_im) / den
    bfr = fr[:, :, None] * b_re - fi[:, :, None] * b_im
    bfi = fr[:, :, None] * b_im + fi[:, :, None] * b_re
    gps = S5_GROUPS // S5_SLABS
    wbu, wc = [], []
    for k in range(S5_SLABS):
        gs = slice(k * gps, (k + 1) * gps)
        wbu.append(jnp.concatenate([_block_diag(jnp.swapaxes(bfr[gs], 1, 2)),
                                    _block_diag(jnp.swapaxes(bfi[gs], 1, 2))], axis=1))
        wc.append(jnp.concatenate([_block_diag(jnp.swapaxes(c_re[gs], 1, 2)),
                                   _block_diag(jnp.swapaxes(-c_im[gs], 1, 2))], axis=0))
    a_pack = jnp.stack([ar.reshape(-1), ai.reshape(-1)])
    return jnp.stack(wbu).astype(_BF), a_pack, jnp.stack(wc).astype(_BF)


def kernel(x, meta_tokens, rel_bias, norm1_g, w_in, kv_norm_g, w_uk, w_uv, lambda_re, lambda_im, log_dt,
           b_re, b_im, c_re, c_im, d_skip, w_glu, gn_a, gn_b, gn_c, w_out, norm2_g, w_mlp1, w_mlp2, final_g):
    bsz, seq, _ = x.shape
    depth = w_in.shape[0]
    n_keys = seq + N_META
    k_top = min(TOPK_MAX, n_keys // 4)
    plen = n_keys + PAD
    rows = bsz * plen
    assert seq % BLOCK == 0 and rows % ROW_TILE == 0 and plen % S5_CHUNK == 0 and bsz % 8 == 0

    head = jnp.concatenate([jnp.zeros((PAD, D_MODEL), x.dtype), meta_tokens.astype(x.dtype)], axis=0)
    h = jnp.concatenate([jnp.broadcast_to(head[None], (bsz, BLOCK, D_MODEL)), x], axis=1).reshape(rows, D_MODEL)

    bucket_t = jnp.swapaxes(_t5_bucket_tiles(), 1, 2)
    ar_ = jnp.arange(BLOCK)
    lower = (ar_[:, None] > ar_[None, :]).astype(_BF)
    ones = jnp.ones((BLOCK, BLOCK), _BF)
    low_ext = jnp.tile(jnp.concatenate([lower, ones], axis=1), (2, 1))
    row1 = lambda v: v.reshape(1, -1).astype(_F32)

    for l in range(depth):
        w_all = _pack_w_in(w_in[l])
        wuk_bd = _block_diag(jnp.transpose(w_uk[l], (1, 2, 0))).astype(_BF)
        wuvt_pad = jnp.stack([jnp.pad(w_uv[l][:, hh, :].T, ((hh * HEAD_DIM, A_WIDTH - (hh + 1) * HEAD_DIM), (0, 0)))
                              for hh in range(A_HEADS)]).astype(_BF)
        qlat4, ckv, iq4, ikw, ub, qc, kc, vc = _inproj(h, row1(norm1_g[l]), w_all, row1(kv_norm_g[l]), wuk_bd)

        ckv3 = ckv.reshape(bsz, plen, KV_RANK)
        ya = _dsa(k_top, rel_bias.astype(_F32), qlat4, iq4, ikw.reshape(bsz, plen, 128),
                  ckv3, jnp.swapaxes(ckv3, 1, 2), bucket_t, wuvt_pad, lower)

        wbu, a_pack, wc = _s5_params(lambda_re[l], lambda_im[l], log_dt[l], b_re[l], b_im[l], c_re[l], c_im[l])
        u_tb = jnp.swapaxes(ub.reshape(bsz, plen, B_WIDTH), 0, 1).reshape(rows, B_WIDTH)
        yb_tb = _s5(bsz, u_tb, wbu, a_pack, wc, row1(d_skip[l]), w_glu[l].astype(_BF))
        yb = jnp.swapaxes(yb_tb.reshape(plen, bsz, B_WIDTH), 0, 1).reshape(rows, B_WIDTH)

        yc = _stick(qc, kc.reshape(bsz, plen, C_WIDTH), vc.reshape(bsz, plen, C_WIDTH), low_ext)

        h = _mix_mlp(plen, l == depth - 1, h, ya, yb, yc, row1(gn_a[l]), row1(gn_b[l]), row1(gn_c[l]),
                     w_out[l].astype(_BF), row1(norm2_g[l]), w_mlp1[l].astype(_BF), w_mlp2[l].astype(_BF),
                     row1(final_g))

    return h.reshape(bsz, plen, D_MODEL)[:, BLOCK:]
```

```python
import functools
import math

import jax
import jax.numpy as jnp
import numpy as np
from jax import lax
from jax.experimental import pallas as pl
from jax.experimental.pallas import tpu as pltpu

D_MODEL = 1024
N_META = 16
BLOCK = 128
PAD = BLOCK - N_META
HEAD_DIM = 64
A_WIDTH = 256
A_HEADS = 4
KV_RANK = 128
IDX_HEADS = 4
IDX_DIM = 64
TOPK_MAX = 256
B_WIDTH = 512
S5_GROUP = 16
S5_GROUPS = 32
S5_STATE = 64
C_WIDTH = 256
C_HEADS = 4
D_FF = 4 * D_MODEL
NUM_BUCKETS = 32
MAX_DISTANCE = 128
RMS_EPS = 1e-6
NEG_INF = -1e30

ROW_TILE = 512
S5_CHUNK = 32
S5_LANES = S5_GROUPS * S5_STATE
S5_SLABS = 4
DSA_GROUP = 4
BISECT_ITERS = 18
STICK_GROUP = 2
VMEM_LIMIT = 56 * 1024 * 1024

_BF = jnp.bfloat16
_F32 = jnp.float32

_C_QA = 0
_C_CKV = _C_QA + A_WIDTH
_C_IQ = _C_CKV + KV_RANK
_C_IKW = _C_IQ + IDX_HEADS * 128
_C_UB = _C_IKW + 128
_C_QC = _C_UB + B_WIDTH
_C_KC = _C_QC + C_WIDTH
_C_VC = _C_KC + C_WIDTH
_N_PROJ = _C_VC + C_WIDTH


def _rms(x, g):
    return x * lax.rsqrt(jnp.mean(x * x, axis=-1, keepdims=True) + RMS_EPS) * g


def _inproj_kernel(h_ref, g1_ref, w_ref, kvg_ref, wuk_ref,
                   qlat_ref, ckv_ref, iq_ref, ikw_ref, ub_ref, qc_ref, kc_ref, vc_ref):
    hn = _rms(h_ref[...], g1_ref[...]).astype(_BF)
    proj = jnp.dot(hn, w_ref[...], preferred_element_type=_F32)
    qa = proj[:, _C_QA:_C_QA + A_WIDTH].astype(_BF)
    qlat = jnp.dot(qa, wuk_ref[...], preferred_element_type=_F32) * (HEAD_DIM ** -0.5)
    for hh in range(A_HEADS):
        qlat_ref[hh] = qlat[:, hh * KV_RANK:(hh + 1) * KV_RANK].astype(_BF)
    ckv_ref[...] = _rms(proj[:, _C_CKV:_C_CKV + KV_RANK], kvg_ref[...]).astype(_BF)
    for hh in range(IDX_HEADS):
        c0 = _C_IQ + hh * 128
        iq_ref[hh] = proj[:, c0:c0 + IDX_DIM].astype(_BF)
    ikw_ref[...] = proj[:, _C_IKW:_C_IKW + 128]
    ub_ref[...] = proj[:, _C_UB:_C_UB + B_WIDTH].astype(_BF)
    qc_ref[...] = (proj[:, _C_QC:_C_QC + C_WIDTH] * (HEAD_DIM ** -0.5)).astype(_BF)
    kc_ref[...] = proj[:, _C_KC:_C_KC + C_WIDTH].astype(_BF)
    vc_ref[...] = proj[:, _C_VC:_C_VC + C_WIDTH].astype(_BF)


def _const_spec(shape):
    nd = len(shape)
    return pl.BlockSpec(shape, lambda *_: (0,) * nd, pipeline_mode=pl.Buffered(1))


def _inproj(h2d, g1, w_all, kvg, wuk_bd):
    rows = h2d.shape[0]
    tm = ROW_TILE
    row = lambda w: pl.BlockSpec((tm, w), lambda i: (i, 0))
    head = lambda w: pl.BlockSpec((4, tm, w), lambda i: (0, i, 0))
    out_shape = (
        jax.ShapeDtypeStruct((A_HEADS, rows, KV_RANK), _BF),
        jax.ShapeDtypeStruct((rows, KV_RANK), _BF),
        jax.ShapeDtypeStruct((IDX_HEADS, rows, IDX_DIM), _BF),
        jax.ShapeDtypeStruct((rows, 128), _F32),
        jax.ShapeDtypeStruct((rows, B_WIDTH), _BF),
        jax.ShapeDtypeStruct((rows, C_WIDTH), _BF),
        jax.ShapeDtypeStruct((rows, C_WIDTH), _BF),
        jax.ShapeDtypeStruct((rows, C_WIDTH), _BF),
    )
    return pl.pallas_call(
        _inproj_kernel,
        grid=(rows // tm,),
        in_specs=[row(D_MODEL), _const_spec((1, D_MODEL)), _const_spec((D_MODEL, _N_PROJ)),
                  _const_spec((1, KV_RANK)), _const_spec((A_WIDTH, A_HEADS * KV_RANK))],
        out_specs=(head(KV_RANK), row(KV_RANK), head(IDX_DIM), row(128), row(B_WIDTH),
                   row(C_WIDTH), row(C_WIDTH), row(C_WIDTH)),
        out_shape=out_shape,
        compiler_params=pltpu.CompilerParams(dimension_semantics=("arbitrary",),
                                             vmem_limit_bytes=VMEM_LIMIT),
        name="inproj",
    )(h2d, g1, w_all, kvg, wuk_bd)


def _dsa_kernel(k_top, nb, relb_ref, qlat_ref, iq_ref, ikw_ref, ckv_ref, ckvt_ref, bucket_ref, wuvt_ref,
                early_ref, ya_ref, score_ref, logit_ref, p_ref, acc_ref, bias_ref):
    grp = DSA_GROUP
    i = pl.program_id(1)
    ngrp = (i + grp) // grp
    kf = float(k_top)
    hw = A_HEADS * BLOCK

    @pl.when(i == 0)
    def _():
        for dd in range(2):
            bk = bucket_ref[dd]
            for hh in range(A_HEADS):
                acc = jnp.zeros((BLOCK, BLOCK), _F32)
                for b_ in range(NUM_BUCKETS):
                    acc = jnp.where(bk == b_, relb_ref[b_, hh], acc)
                bias_ref[dd, :, hh * BLOCK:(hh + 1) * BLOCK] = acc
        for hh in range(A_HEADS):
            bias_ref[2, :, hh * BLOCK:(hh + 1) * BLOCK] = jnp.full((BLOCK, BLOCK), relb_ref[NUM_BUCKETS - 1, hh], _F32)

    key_s = lax.broadcasted_iota(jnp.int32, (BLOCK, BLOCK), 0)
    q_t = i * BLOCK + lax.broadcasted_iota(jnp.int32, (BLOCK, BLOCK), 1)

    def blocks_of(t):
        return [grp * t + g for g in range(grp)]

    def rows_of(j):
        return pl.ds(pl.multiple_of(jnp.minimum(j, nb - 1) * BLOCK, BLOCK), BLOCK)

    def col_sum(x):
        return jnp.sum(x, axis=0, keepdims=True)

    def fold8(x, op):
        return op(x.reshape(BLOCK // 8, 8, BLOCK), axis=0)

    iq = iq_ref[...].reshape(hw, IDX_DIM)
    q4 = qlat_ref[...].reshape(hw, KV_RANK)
    wq_t = ikw_ref[pl.ds(pl.multiple_of(i * BLOCK, BLOCK), BLOCK), :].T
    wrow = [wq_t[IDX_DIM + hh:IDX_DIM + hh + 1, :] * (IDX_DIM ** -0.5 * IDX_HEADS ** -0.5)
            for hh in range(IDX_HEADS)]
    nt = (((1,), (1,)), ((), ()))

    def score_body(t, carry):
        mn, mx = carry
        js = blocks_of(t)
        iks = [ikw_ref[rows_of(j), :][:, :IDX_DIM].astype(_BF) for j in js]
        ckvs = [ckv_ref[rows_of(j), :] for j in js]
        raws = [lax.dot_general(ik, iq, nt, preferred_element_type=_F32) for ik in iks]
        lts = [lax.dot_general(ckv, q4, nt, preferred_element_type=_F32) for ckv in ckvs]
        outs = []
        for j, raw in zip(js, raws):
            s = jnp.zeros((BLOCK, BLOCK), _F32)
            for hh in range(IDX_HEADS):
                s = s + jnp.maximum(raw[:, hh * BLOCK:(hh + 1) * BLOCK], 0.0) * wrow[hh]
            s = jnp.where(s == 0.0, 0.0, s)
            kpos = j * BLOCK + key_s
            adm = (kpos <= q_t) & (kpos >= PAD)
            s_lo = jnp.where(adm, s, -jnp.inf)
            mn = jnp.minimum(mn, fold8(jnp.where(adm, s, jnp.inf), jnp.min))
            mx = jnp.maximum(mx, fold8(s_lo, jnp.max))
            outs.append(s_lo)
        for j, s_lo, lt in zip(js, outs, lts):
            score_ref[j] = s_lo
            logit_ref[j] = lt
        return mn, mx

    mn, mx = lax.fori_loop(0, ngrp, score_body, (jnp.full((8, BLOCK), jnp.inf, _F32),
                                                 jnp.full((8, BLOCK), -jnp.inf, _F32)))
    s_min = jnp.min(mn, axis=0, keepdims=True)
    s_max = jnp.max(mx, axis=0, keepdims=True)

    n_adm = (i * BLOCK + lax.broadcasted_iota(jnp.int32, (1, BLOCK), 1) - (PAD - 1)).astype(_F32)
    take_all = n_adm <= kf

    def count_where(pred):
        def body(t, acc):
            for j in blocks_of(t):
                acc = acc + jnp.where(pred(score_ref[j]), 1.0, 0.0)
            return acc
        return col_sum(lax.fori_loop(0, ngrp, body, jnp.zeros((BLOCK, BLOCK), _F32)))

    def max_where(pred):
        def body(t, m):
            for j in blocks_of(t):
                s = score_ref[j]
                m = jnp.maximum(m, jnp.where(pred(s), s, -jnp.inf))
            return m
        m = lax.fori_loop(0, ngrp, body, jnp.full((BLOCK, BLOCK), -jnp.inf, _F32))
        return jnp.max(m, axis=0, keepdims=True)

    def select_threshold():
        def bis(_, st):
            lo, hi, c_hi = st
            mid = (lo + hi) * 0.5
            c = count_where(lambda s: s > mid)
            take = c < kf
            return (jnp.where(take, lo, mid), jnp.where(take, mid, hi), jnp.where(take, c, c_hi))

        _, hi_f, c_gt = lax.fori_loop(0, BISECT_ITERS, bis, (s_min, s_max, jnp.zeros((1, BLOCK), _F32)))
        thr = max_where(lambda s: s <= hi_f)
        g = count_where(lambda s: s == thr)

        def not_done(st):
            thr, c_gt, g, it = st
            pending = jnp.where(take_all, 0.0, jnp.where(c_gt + g >= kf, 0.0, 1.0))
            return jnp.logical_and(jnp.max(pending) > 0.0, it < 4096)

        def peel(st):
            thr, c_gt, g, it = st
            done = jnp.logical_or(take_all, c_gt + g >= kf)
            thr_n = max_where(lambda s: s < thr)
            g_n = count_where(lambda s: s == thr_n)
            return (jnp.where(done, thr, thr_n), jnp.where(done, c_gt, c_gt + g),
                    jnp.where(done, g, g_n), it + 1)

        thr, c_gt, g, _ = lax.while_loop(not_done, peel, (thr, c_gt, g, jnp.int32(0)))
        return thr, c_gt, g

    zrow = jnp.zeros((1, BLOCK), _F32)
    thr, c_gt, g = lax.cond(i * BLOCK + (BLOCK - 1) - (PAD - 1) > k_top, select_threshold,
                            lambda: (zrow, zrow, zrow))
    thr = jnp.where(take_all, -jnp.inf, thr)
    need = kf - c_gt
    surplus_ties = jnp.max(jnp.where(take_all, 0.0, jnp.where(c_gt + g > kf, 1.0, 0.0))) > 0.0

    def mask_pass(rank_ties):
        def body(t, st):
            eq_before, m8 = st
            js = blocks_of(t)
            ss = [score_ref[j] for j in js]
            lts = [logit_ref[j] for j in js]
            biases = [bias_ref[jnp.clip(i - j, 0, 2)] for j in js]
            m8 = list(m8)
            blks = []
            for s, lt, bias in zip(ss, lts, biases):
                if rank_ties:
                    eq = (s == thr) & (s > -jnp.inf)
                    eqf = jnp.where(eq, 1.0, 0.0)
                    rank = eq_before + jnp.dot(early_ref[...], eqf.astype(_BF), preferred_element_type=_F32)
                    sel = (s > thr) | (eq & (rank < need))
                    eq_before = eq_before + col_sum(eqf)
                else:
                    sel = (s >= thr) & (s > -jnp.inf)
                row = []
                for hh in range(A_HEADS):
                    hs = slice(hh * BLOCK, (hh + 1) * BLOCK)
                    blk = jnp.where(sel, lt[:, hs] + bias[:, hs], NEG_INF)
                    m8[hh] = jnp.maximum(m8[hh], fold8(blk, jnp.max))
                    row.append(blk)
                blks.append(row)
            for j, row in zip(js, blks):
                for hh in range(A_HEADS):
                    logit_ref[j, :, hh * BLOCK:(hh + 1) * BLOCK] = row[hh]
            return eq_before, m8

        _, m8 = lax.fori_loop(0, ngrp, body, (zrow, [jnp.full((8, BLOCK), NEG_INF, _F32)] * A_HEADS))
        return m8

    m8 = lax.cond(surplus_ties, lambda: mask_pass(True), lambda: mask_pass(False))
    m = jnp.concatenate([jnp.max(x, axis=0, keepdims=True) for x in m8], axis=1)

    p_ref[1] = jnp.zeros(p_ref.shape[1:], _BF)
    acc_ref[...] = jnp.zeros_like(acc_ref)

    def pv_body(t, l8):
        slot = lax.rem(t, 2)
        acc = acc_ref[...] + jnp.dot(ckvt_ref[jnp.maximum(t - 1, 0)], p_ref[1 - slot], preferred_element_type=_F32)
        ps = [jnp.exp(logit_ref[j] - m) for j in blocks_of(t)]
        l8 = [l8[hh] + sum(fold8(p[:, hh * BLOCK:(hh + 1) * BLOCK], jnp.sum) for p in ps) for hh in range(A_HEADS)]
        acc_ref[...] = acc
        p_ref[slot] = jnp.concatenate([p.astype(_BF) for p in ps], axis=0)
        return l8

    l8 = lax.fori_loop(0, ngrp, pv_body, [jnp.zeros((8, BLOCK), _F32)] * A_HEADS)
    acc = acc_ref[...] + jnp.dot(ckvt_ref[ngrp - 1], p_ref[lax.rem(ngrp - 1, 2)], preferred_element_type=_F32)
    l = jnp.concatenate([col_sum(x) for x in l8], axis=1)
    o_lat = (acc / l).astype(_BF)
    out_t = jnp.zeros((A_WIDTH, BLOCK), _F32)
    for hh in range(A_HEADS):
        out_t = out_t + jnp.dot(wuvt_ref[hh], o_lat[:, hh * BLOCK:(hh + 1) * BLOCK], preferred_element_type=_F32)
    ya_ref[...] = out_t.T.astype(_BF)


def _dsa(k_top, rel_bias, qlat4, iq4, ikw3, ckv3, ckvt3, bucket_t, wuvt_pad, early):
    bsz, plen, _ = ckv3.shape
    nb = plen // BLOCK
    ngmax = (nb + DSA_GROUP - 1) // DSA_GROUP
    nblk = ngmax * DSA_GROUP
    hw = A_HEADS * BLOCK
    assert ckvt3.shape == (bsz, ngmax, KV_RANK, DSA_GROUP * BLOCK)
    return pl.pallas_call(
        functools.partial(_dsa_kernel, k_top, nb),
        grid=(bsz, nb),
        in_specs=[
            pl.BlockSpec(memory_space=pltpu.SMEM),
            pl.BlockSpec((A_HEADS, BLOCK, KV_RANK), lambda b, i: (0, b * nb + i, 0)),
            pl.BlockSpec((IDX_HEADS, BLOCK, IDX_DIM), lambda b, i: (0, b * nb + i, 0)),
            pl.BlockSpec((None, plen, 128), lambda b, i: (b, 0, 0)),
            pl.BlockSpec((None, plen, KV_RANK), lambda b, i: (b, 0, 0)),
            pl.BlockSpec((None, ngmax, KV_RANK, DSA_GROUP * BLOCK), lambda b, i: (b, 0, 0, 0)),
            _const_spec((2, BLOCK, BLOCK)),
            _const_spec((A_HEADS, A_WIDTH, KV_RANK)),
            _const_spec((BLOCK, BLOCK)),
        ],
        out_specs=pl.BlockSpec((BLOCK, A_WIDTH), lambda b, i: (b * nb + i, 0)),
        out_shape=jax.ShapeDtypeStruct((bsz * plen, A_WIDTH), _BF),
        scratch_shapes=[pltpu.VMEM((nblk, BLOCK, BLOCK), _F32),
                        pltpu.VMEM((nblk, BLOCK, hw), _F32),
                        pltpu.VMEM((2, DSA_GROUP * BLOCK, hw), _BF),
                        pltpu.VMEM((KV_RANK, hw), _F32),
                        pltpu.VMEM((3, BLOCK, hw), _F32)],
        compiler_params=pltpu.CompilerParams(dimension_semantics=("arbitrary", "arbitrary"),
                                             vmem_limit_bytes=VMEM_LIMIT),
        name="dsa",
    )(rel_bias, qlat4, iq4, ikw3, ckv3, ckvt3, bucket_t, wuvt_pad, early)


def _stick_kernel(nb, q_ref, k_ref, v_ref, low_ref, yc_ref, z_ref, w_ref, carry_ref, acc_ref):
    grp = STICK_GROUP
    i = pl.program_id(1)
    ngrp = (i + grp) // grp
    row_t = i * BLOCK + lax.broadcasted_iota(jnp.int32, (BLOCK, BLOCK), 0)
    lane = lax.broadcasted_iota(jnp.int32, (BLOCK, BLOCK), 1)
    first_head = lane < HEAD_DIM

    qm = []
    for hh in range(C_HEADS):
        qp = q_ref[:, (hh // 2) * BLOCK:(hh // 2 + 1) * BLOCK].astype(_F32)
        keep = first_head if hh % 2 == 0 else jnp.logical_not(first_head)
        qm.append(jnp.where(keep, qp, 0.0).astype(_BF))

    def blocks_of(t):
        base = (ngrp - 1 - t) * grp
        return [base + (grp - 1 - g) for g in range(grp)]

    def rows_of(j):
        return pl.ds(pl.multiple_of(jnp.clip(j, 0, nb - 1) * BLOCK, BLOCK), BLOCK)

    def qk(t):
        out = []
        for j in blocks_of(t):
            kps = [k_ref[rows_of(j), pr * BLOCK:(pr + 1) * BLOCK] for pr in range(C_HEADS // 2)]
            out += [lax.dot_general(qm[hh], kps[hh // 2], (((1,), (1,)), ((), ())),
                                    preferred_element_type=_F32) for hh in range(C_HEADS)]
        return out

    def wv(t, slot, acc):
        js = blocks_of(t)
        vs = [[v_ref[rows_of(j), pr * BLOCK:(pr + 1) * BLOCK] for j in js] for pr in range(C_HEADS // 2)]
        new = []
        for hh in range(C_HEADS):
            wcat = jnp.concatenate([w_ref[slot, g * C_HEADS + hh] for g in range(grp)], axis=1)
            vcat = jnp.concatenate(vs[hh // 2], axis=0)
            new.append(acc[hh] + jnp.dot(wcat, vcat, preferred_element_type=_F32))
        return new

    for n, z in enumerate(qk(0)):
        z_ref[0, n] = z
    w_ref[1] = jnp.zeros(w_ref.shape[1:], _BF)
    carry_ref[...] = jnp.zeros_like(carry_ref)
    acc_ref[...] = jnp.zeros_like(acc_ref)

    def body(t, _):
        slot = lax.rem(t, 2)
        acc = wv(t - 1, 1 - slot, [acc_ref[hh] for hh in range(C_HEADS)])
        z_next = qk(t + 1)
        carry = [carry_ref[hh] for hh in range(C_HEADS)]
        zs = [z_ref[slot, n] for n in range(grp * C_HEADS)]

        lks, zzs, masks = [], [], []
        for g, j in enumerate(blocks_of(t)):
            kpos = j * BLOCK + lane
            strict = (kpos < row_t) & (kpos >= PAD)
            masks.append(strict)
            for hh in range(C_HEADS):
                z = zs[g * C_HEADS + hh]
                zz = jnp.minimum(z, 0.0) - jnp.log(1.0 + jnp.exp(-jnp.abs(z)))
                lks.append(jnp.where(strict, zz - z, 0.0).astype(_BF))
                zzs.append(zz)
        cums = [jnp.dot(lk, low_ref[...], preferred_element_type=_F32) for lk in lks]
        ws = []
        for g in range(grp):
            for hh in range(C_HEADS):
                n = g * C_HEADS + hh
                w = jnp.where(masks[g], jnp.exp(zzs[n] + cums[n][:, :BLOCK] + carry[hh]), 0.0)
                ws.append(w.astype(_BF))
                carry[hh] = carry[hh] + cums[n][:, BLOCK:]

        for hh in range(C_HEADS):
            acc_ref[hh] = acc[hh]
            carry_ref[hh] = carry[hh]
        for n in range(grp * C_HEADS):
            z_ref[1 - slot, n] = z_next[n]
            w_ref[slot, n] = ws[n]
        return 0

    lax.fori_loop(0, ngrp, body, 0)
    acc = wv(ngrp - 1, lax.rem(ngrp - 1, 2), [acc_ref[hh] for hh in range(C_HEADS)])
    yc_ref[...] = jnp.concatenate([jnp.where(first_head, acc[2 * pr], acc[2 * pr + 1])
                                   for pr in range(C_HEADS // 2)], axis=1).astype(_BF)


def _stick(qc2d, kc3, vc3, low_ext):
    bsz, plen, _ = kc3.shape
    nb = plen // BLOCK
    nchain = STICK_GROUP * C_HEADS
    return pl.pallas_call(
        functools.partial(_stick_kernel, nb),
        grid=(bsz, nb),
        in_specs=[
            pl.BlockSpec((BLOCK, C_WIDTH), lambda b, i: (b * nb + i, 0)),
            pl.BlockSpec((None, plen, C_WIDTH), lambda b, i: (b, 0, 0)),
            pl.BlockSpec((None, plen, C_WIDTH), lambda b, i: (b, 0, 0)),
            _const_spec((BLOCK, 2 * BLOCK)),
        ],
        out_specs=pl.BlockSpec((BLOCK, C_WIDTH), lambda b, i: (b * nb + i, 0)),
        out_shape=jax.ShapeDtypeStruct((bsz * plen, C_WIDTH), _BF),
        scratch_shapes=[pltpu.VMEM((2, nchain, BLOCK, BLOCK), _F32),
                        pltpu.VMEM((2, nchain, BLOCK, BLOCK), _BF),
                        pltpu.VMEM((C_HEADS, BLOCK, BLOCK), _F32),
                        pltpu.VMEM((C_HEADS, BLOCK, BLOCK), _F32)],
        compiler_params=pltpu.CompilerParams(dimension_semantics=("arbitrary", "arbitrary"),
                                             vmem_limit_bytes=VMEM_LIMIT),
        name="stick",
    )(qc2d, kc3, vc3, low_ext)


def _s5_kernel(bsz, u_ref, wbu_ref, a_ref, wc_ref, d_ref, wglu_ref, y_ref, bur_ref, bui_ref, st_ref):
    c = pl.program_id(0)
    slab_w = S5_LANES // S5_SLABS
    ch_w = B_WIDTH // S5_SLABS

    @pl.when(c == 0)
    def _():
        st_ref[...] = jnp.zeros_like(st_ref)

    u = u_ref[...]
    for k in range(S5_SLABS):
        o = jnp.dot(u[:, k * ch_w:(k + 1) * ch_w], wbu_ref[k], preferred_element_type=_F32)
        bur_ref[:, k * slab_w:(k + 1) * slab_w] = o[:, :slab_w]
        bui_ref[:, k * slab_w:(k + 1) * slab_w] = o[:, slab_w:]

    for k in range(S5_SLABS):
        ls = slice(k * slab_w, (k + 1) * slab_w)
        ar = jnp.broadcast_to(a_ref[0:1, ls], (bsz, slab_w))
        ai = jnp.broadcast_to(a_ref[1:2, ls], (bsz, slab_w))

        def step(t, st, ls=ls, ar=ar, ai=ai):
            hr, hi = st
            rows = pl.ds(pl.multiple_of(t * bsz, bsz), bsz)
            nhr = ar * hr - ai * hi + bur_ref[rows, ls]
            nhi = ar * hi + ai * hr + bui_ref[rows, ls]
            bur_ref[rows, ls] = nhr
            bui_ref[rows, ls] = nhi
            return nhr, nhi

        hr, hi = lax.fori_loop(0, S5_CHUNK, step, (st_ref[0, :, ls], st_ref[1, :, ls]), unroll=4)
        st_ref[0, :, ls] = hr
        st_ref[1, :, ls] = hi

    ys = []
    for k in range(S5_SLABS):
        ls = slice(k * slab_w, (k + 1) * slab_w)
        hcat = jnp.concatenate([bur_ref[:, ls].astype(_BF), bui_ref[:, ls].astype(_BF)], axis=1)
        ys.append(jnp.dot(hcat, wc_ref[k], preferred_element_type=_F32))
    y = jnp.concatenate(ys, axis=1) + d_ref[...] * u.astype(_F32)
    y = 0.5 * y * (1.0 + jnp.tanh(math.sqrt(2.0 / math.pi) * (y + 0.044715 * (y * y * y))))
    gate = jnp.dot(y.astype(_BF), wglu_ref[...], preferred_element_type=_F32)
    y_ref[...] = (y * (1.0 / (1.0 + jnp.exp(-gate)))).astype(_BF)


def _s5(bsz, u_tb, wbu, a_pack, wc, d_skip, w_glu):
    rows = u_tb.shape[0]
    rc = S5_CHUNK * bsz
    return pl.pallas_call(
        functools.partial(_s5_kernel, bsz),
        grid=(rows // rc,),
        in_specs=[
            pl.BlockSpec((rc, B_WIDTH), lambda c: (c, 0)),
            _const_spec((S5_SLABS, B_WIDTH // S5_SLABS, 2 * S5_LANES // S5_SLABS)),
            _const_spec((2, S5_LANES)),
            _const_spec((S5_SLABS, 2 * S5_LANES // S5_SLABS, B_WIDTH // S5_SLABS)),
            _const_spec((1, B_WIDTH)),
            _const_spec((B_WIDTH, B_WIDTH)),
        ],
        out_specs=pl.BlockSpec((rc, B_WIDTH), lambda c: (c, 0)),
        out_shape=jax.ShapeDtypeStruct((rows, B_WIDTH), _BF),
        scratch_shapes=[pltpu.VMEM((rc, S5_LANES), _F32), pltpu.VMEM((rc, S5_LANES), _F32),
                        pltpu.VMEM((2, bsz, S5_LANES), _F32)],
        compiler_params=pltpu.CompilerParams(dimension_semantics=("arbitrary",),
                                             vmem_limit_bytes=VMEM_LIMIT),
        name="s5",
    )(u_tb, wbu, a_pack, wc, d_skip, w_glu)


def _mix_mlp_kernel(plen, final, h_ref, ya_ref, yb_ref, yc_ref, ga_ref, gb_ref, gc_ref, wo_ref,
                    g2_ref, w1_ref, w2_ref, gf_ref, o_ref):
    tm = h_ref.shape[0]
    y = jnp.concatenate([_rms(ya_ref[...].astype(_F32), ga_ref[...]),
                         _rms(yb_ref[...].astype(_F32), gb_ref[...]),
                         _rms(yc_ref[...].astype(_F32), gc_ref[...])], axis=1).astype(_BF)
    h = h_ref[...] + jnp.dot(y, wo_ref[...], preferred_element_type=_F32)
    hn = _rms(h, g2_ref[...]).astype(_BF)
    ffc = D_FF // 4
    for cc in range(4):
        a = jnp.dot(hn, w1_ref[:, cc * ffc:(cc + 1) * ffc], preferred_element_type=_F32)
        a = jnp.square(jnp.maximum(a, 0.0)).astype(_BF)
        h = h + jnp.dot(a, w2_ref[cc * ffc:(cc + 1) * ffc, :], preferred_element_type=_F32)
    if final:
        h = _rms(h, gf_ref[...])
    r = (pl.program_id(0) * tm + lax.broadcasted_iota(jnp.int32, (tm, 1), 0)).astype(_F32)
    pos = r - jnp.floor((r + 0.5) * (1.0 / plen)) * plen
    o_ref[...] = jnp.where(pos >= PAD, h, 0.0)


def _mix_mlp(plen, final, h2d, ya, yb, yc, ga, gb, gc, wo, g2, w1, w2, gf):
    rows = h2d.shape[0]
    tm = ROW_TILE
    row = lambda w: pl.BlockSpec((tm, w), lambda i: (i, 0))
    return pl.pallas_call(
        functools.partial(_mix_mlp_kernel, plen, final),
        grid=(rows // tm,),
        in_specs=[row(D_MODEL), row(A_WIDTH), row(B_WIDTH), row(C_WIDTH),
                  _const_spec((1, A_WIDTH)), _const_spec((1, B_WIDTH)), _const_spec((1, C_WIDTH)),
                  _const_spec((D_MODEL, D_MODEL)), _const_spec((1, D_MODEL)),
                  _const_spec((D_MODEL, D_FF)), _const_spec((D_FF, D_MODEL)), _const_spec((1, D_MODEL))],
        out_specs=row(D_MODEL),
        out_shape=jax.ShapeDtypeStruct((rows, D_MODEL), _F32),
        compiler_params=pltpu.CompilerParams(dimension_semantics=("arbitrary",),
                                             vmem_limit_bytes=VMEM_LIMIT),
        name="mix_mlp",
    )(h2d, ya, yb, yc, ga, gb, gc, wo, g2, w1, w2, gf)


def _t5_bucket_tiles():
    max_exact = NUM_BUCKETS // 2
    r = jnp.arange(BLOCK)[:, None]
    c = jnp.arange(BLOCK)[None, :]
    tiles = []
    for dd in range(2):
        n = jnp.maximum(dd * BLOCK + r - c, 0)
        nf = jnp.maximum(n, max_exact).astype(_F32)
        large = max_exact + (jnp.log(nf / max_exact) / math.log(MAX_DISTANCE / max_exact)
                             * (NUM_BUCKETS - max_exact)).astype(jnp.int32)
        large = jnp.minimum(large, NUM_BUCKETS - 1)
        tiles.append(jnp.where(n < max_exact, n, large))
    return jnp.stack(tiles).astype(jnp.int32)


def _pack_w_in(w_in):
    sizes = (A_WIDTH, KV_RANK, IDX_HEADS * IDX_DIM, IDX_DIM, IDX_HEADS, B_WIDTH, C_WIDTH, C_WIDTH, C_WIDTH)
    offs = np.cumsum((0,) + sizes)
    qa, ckv, iq, ik, iw, ub, qc, kc, vc = [w_in[:, offs[n]:offs[n + 1]] for n in range(len(sizes))]
    zeros = lambda n: jnp.zeros((w_in.shape[0], n), w_in.dtype)
    cols = [qa, ckv]
    for hh in range(IDX_HEADS):
        cols += [iq[:, hh * IDX_DIM:(hh + 1) * IDX_DIM], zeros(128 - IDX_DIM)]
    cols += [ik, iw, zeros(128 - IDX_DIM - IDX_HEADS), ub, qc, kc, vc]
    return jnp.concatenate(cols, axis=1).astype(_BF)


def _block_diag(blocks):
    n, r, c = blocks.shape
    eye = jnp.eye(n, dtype=blocks.dtype)
    return (eye[:, None, :, None] * blocks[:, :, None, :]).reshape(n * r, n * c)


def _s5_params(lam_re, lam_im, log_dt, b_re, b_im, c_re, c_im):
    dt = jnp.exp(log_dt)[:, None]
    mag = jnp.exp(lam_re * dt)
    ar = mag * jnp.cos(lam_im * dt)
    ai = mag * jnp.sin(lam_im * dt)
    den = lam_re * lam_re + lam_im * lam_im
    nr, ni = ar - 1.0, ai
    fr = (nr * lam_re + ni * lam_im) / den
    fi = (ni * lam_re - nr * lam_im) / den
    bfr = fr[:, :, None] * b_re - fi[:, :, None] * b_im
    bfi = fr[:, :, None] * b_im + fi[:, :, None] * b_re
    gps = S5_GROUPS // S5_SLABS
    wbu, wc = [], []
    for k in range(S5_SLABS):
        gs = slice(k * gps, (k + 1) * gps)
        wbu.append(jnp.concatenate([_block_diag(jnp.swapaxes(bfr[gs], 1, 2)),
                                    _block_diag(jnp.swapaxes(bfi[gs], 1, 2))], axis=1))
        wc.append(jnp.concatenate([_block_diag(jnp.swapaxes(c_re[gs], 1, 2)),
                                   _block_diag(jnp.swapaxes(-c_im[gs], 1, 2))], axis=0))
    a_pack = jnp.stack([ar.reshape(-1), ai.reshape(-1)])
    return jnp.stack(wbu).astype(_BF), a_pack, jnp.stack(wc).astype(_BF)


def kernel(x, meta_tokens, rel_bias, norm1_g, w_in, kv_norm_g, w_uk, w_uv, lambda_re, lambda_im, log_dt,
           b_re, b_im, c_re, c_im, d_skip, w_glu, gn_a, gn_b, gn_c, w_out, norm2_g, w_mlp1, w_mlp2, final_g):
    bsz, seq, _ = x.shape
    depth = w_in.shape[0]
    n_keys = seq + N_META
    k_top = min(TOPK_MAX, n_keys // 4)
    plen = n_keys + PAD
    rows = bsz * plen
    assert seq % BLOCK == 0 and rows % ROW_TILE == 0 and plen % S5_CHUNK == 0 and bsz % 8 == 0

    head = jnp.concatenate([jnp.zeros((PAD, D_MODEL), x.dtype), meta_tokens.astype(x.dtype)], axis=0)
    h = jnp.concatenate([jnp.broadcast_to(head[None], (bsz, BLOCK, D_MODEL)), x], axis=1).reshape(rows, D_MODEL)

    bucket_t = jnp.swapaxes(_t5_bucket_tiles(), 1, 2)
    ar_ = jnp.arange(BLOCK)
    lower = (ar_[:, None] > ar_[None, :]).astype(_BF)
    ones = jnp.ones((BLOCK, BLOCK), _BF)
    low_ext = jnp.concatenate([lower, ones], axis=1)
    row1 = lambda v: v.reshape(1, -1).astype(_F32)

    for l in range(depth):
        w_all = _pack_w_in(w_in[l])
        wuk_bd = _block_diag(jnp.transpose(w_uk[l], (1, 2, 0))).astype(_BF)
        wuvt_pad = jnp.stack([jnp.pad(w_uv[l][:, hh, :].T, ((hh * HEAD_DIM, A_WIDTH - (hh + 1) * HEAD_DIM), (0, 0)))
                              for hh in range(A_HEADS)]).astype(_BF)
        qlat4, ckv, iq4, ikw, ub, qc, kc, vc = _inproj(h, row1(norm1_g[l]), w_all, row1(kv_norm_g[l]), wuk_bd)

        ckv3 = ckv.reshape(bsz, plen, KV_RANK)
        gk = DSA_GROUP * BLOCK
        ckvt3 = jnp.swapaxes(jnp.pad(ckv3, ((0, 0), (0, -plen % gk), (0, 0))).reshape(bsz, -1, gk, KV_RANK), 2, 3)
        ya = _dsa(k_top, rel_bias.astype(_F32), qlat4, iq4, ikw.reshape(bsz, plen, 128),
                  ckv3, ckvt3, bucket_t, wuvt_pad, lower)

        wbu, a_pack, wc = _s5_params(lambda_re[l], lambda_im[l], log_dt[l], b_re[l], b_im[l], c_re[l], c_im[l])
        u_tb = jnp.swapaxes(ub.reshape(bsz, plen, B_WIDTH), 0, 1).reshape(rows, B_WIDTH)
        yb_tb = _s5(bsz, u_tb, wbu, a_pack, wc, row1(d_skip[l]), w_glu[l].astype(_BF))
        yb = jnp.swapaxes(yb_tb.reshape(plen, bsz, B_WIDTH), 0, 1).reshape(rows, B_WIDTH)

        yc = _stick(qc, kc.reshape(bsz, plen, C_WIDTH), vc.reshape(bsz, plen, C_WIDTH), low_ext)

        h = _mix_mlp(plen, l == depth - 1, h, ya, yb, yc, row1(gn_a[l]), row1(gn_b[l]), row1(gn_c[l]),
                     w_out[l].astype(_BF), row1(norm2_g[l]), w_mlp1[l].astype(_BF), w_mlp2[l].astype(_BF),
                     row1(final_g))

    return h.reshape(bsz, plen, D_MODEL)[:, BLOCK:]
```

```python
import functools
import math

import jax
import jax.numpy as jnp
import numpy as np
from jax import lax
from jax.experimental import pallas as pl
from jax.experimental.pallas import tpu as pltpu

D_MODEL = 1024
N_META = 16
BLOCK = 128
PAD = BLOCK - N_META
HEAD_DIM = 64
A_WIDTH = 256
A_HEADS = 4
KV_RANK = 128
IDX_HEADS = 4
IDX_DIM = 64
TOPK_MAX = 256
B_WIDTH = 512
S5_GROUP = 16
S5_GROUPS = 32
S5_STATE = 64
C_WIDTH = 256
C_HEADS = 4
D_FF = 4 * D_MODEL
NUM_BUCKETS = 32
MAX_DISTANCE = 128
RMS_EPS = 1e-6
NEG_INF = -1e30

ROW_TILE = 512
S5_CHUNK = 32
S5_LANES = S5_GROUPS * S5_STATE
S5_SLABS = 4
STEP_BATCH = 2
DSA_GROUP = 2
BISECT_ITERS = 18
STICK_GROUP = 1
VMEM_LIMIT = 56 * 1024 * 1024

_BF = jnp.bfloat16
_F32 = jnp.float32

_C_QA = 0
_C_CKV = _C_QA + A_WIDTH
_C_IQ = _C_CKV + KV_RANK
_C_IKW = _C_IQ + IDX_HEADS * 128
_C_UB = _C_IKW + 128
_C_QC = _C_UB + B_WIDTH
_C_KC = _C_QC + C_WIDTH
_C_VC = _C_KC + C_WIDTH
_N_PROJ = _C_VC + C_WIDTH


def _rms(x, g):
    return x * lax.rsqrt(jnp.mean(x * x, axis=-1, keepdims=True) + RMS_EPS) * g


def _inproj_kernel(h_ref, g1_ref, w_ref, kvg_ref, wuk_ref,
                   qlat_ref, ckv_ref, iq_ref, ikw_ref, ub_ref, qc_ref, kc_ref, vc_ref):
    hn = _rms(h_ref[...], g1_ref[...]).astype(_BF)
    proj = jnp.dot(hn, w_ref[...], preferred_element_type=_F32)
    qa = proj[:, _C_QA:_C_QA + A_WIDTH].astype(_BF)
    qlat = jnp.dot(qa, wuk_ref[...], preferred_element_type=_F32) * (HEAD_DIM ** -0.5)
    for hh in range(A_HEADS):
        qlat_ref[hh] = qlat[:, hh * KV_RANK:(hh + 1) * KV_RANK].astype(_BF)
    ckv_ref[...] = _rms(proj[:, _C_CKV:_C_CKV + KV_RANK], kvg_ref[...]).astype(_BF)
    for hh in range(IDX_HEADS):
        c0 = _C_IQ + hh * 128
        iq_ref[hh] = proj[:, c0:c0 + IDX_DIM].astype(_BF)
    ikw_ref[...] = proj[:, _C_IKW:_C_IKW + 128]
    ub_ref[...] = proj[:, _C_UB:_C_UB + B_WIDTH].astype(_BF)
    qc_ref[...] = (proj[:, _C_QC:_C_QC + C_WIDTH] * (HEAD_DIM ** -0.5)).astype(_BF)
    kc_ref[...] = proj[:, _C_KC:_C_KC + C_WIDTH].astype(_BF)
    vc_ref[...] = proj[:, _C_VC:_C_VC + C_WIDTH].astype(_BF)


def _const_spec(shape):
    nd = len(shape)
    return pl.BlockSpec(shape, lambda *_: (0,) * nd, pipeline_mode=pl.Buffered(1))


def _inproj(h2d, g1, w_all, kvg, wuk_bd):
    rows = h2d.shape[0]
    tm = ROW_TILE
    row = lambda w: pl.BlockSpec((tm, w), lambda i: (i, 0))
    head = lambda w: pl.BlockSpec((4, tm, w), lambda i: (0, i, 0))
    out_shape = (
        jax.ShapeDtypeStruct((A_HEADS, rows, KV_RANK), _BF),
        jax.ShapeDtypeStruct((rows, KV_RANK), _BF),
        jax.ShapeDtypeStruct((IDX_HEADS, rows, IDX_DIM), _BF),
        jax.ShapeDtypeStruct((rows, 128), _F32),
        jax.ShapeDtypeStruct((rows, B_WIDTH), _BF),
        jax.ShapeDtypeStruct((rows, C_WIDTH), _BF),
        jax.ShapeDtypeStruct((rows, C_WIDTH), _BF),
        jax.ShapeDtypeStruct((rows, C_WIDTH), _BF),
    )
    return pl.pallas_call(
        _inproj_kernel,
        grid=(rows // tm,),
        in_specs=[row(D_MODEL), _const_spec((1, D_MODEL)), _const_spec((D_MODEL, _N_PROJ)),
                  _const_spec((1, KV_RANK)), _const_spec((A_WIDTH, A_HEADS * KV_RANK))],
        out_specs=(head(KV_RANK), row(KV_RANK), head(IDX_DIM), row(128), row(B_WIDTH),
                   row(C_WIDTH), row(C_WIDTH), row(C_WIDTH)),
        out_shape=out_shape,
        compiler_params=pltpu.CompilerParams(dimension_semantics=("arbitrary",),
                                             vmem_limit_bytes=VMEM_LIMIT),
        name="inproj",
    )(h2d, g1, w_all, kvg, wuk_bd)


def _dsa_kernel(k_top, nb, relb_ref, qlat_ref, iq_ref, ikw_ref, ckv_ref, ckvt_ref, bucket_ref, wuvt_ref,
                early_ref, ya_ref, score_ref, logit_ref, p_ref, acc_ref, bias_ref):
    grp = DSA_GROUP
    nbt = STEP_BATCH
    i = pl.program_id(1)
    ngrp = (i + grp) // grp
    kf = float(k_top)
    hw = A_HEADS * BLOCK
    bbs = range(nbt)

    @pl.when(i == 0)
    def _():
        for dd in range(2):
            bk = bucket_ref[dd]
            for hh in range(A_HEADS):
                acc = jnp.zeros((BLOCK, BLOCK), _F32)
                for b_ in range(NUM_BUCKETS):
                    acc = jnp.where(bk == b_, relb_ref[b_, hh], acc)
                bias_ref[dd, :, hh * BLOCK:(hh + 1) * BLOCK] = acc
        for hh in range(A_HEADS):
            bias_ref[2, :, hh * BLOCK:(hh + 1) * BLOCK] = jnp.full((BLOCK, BLOCK), relb_ref[NUM_BUCKETS - 1, hh], _F32)

    key_s = lax.broadcasted_iota(jnp.int32, (BLOCK, BLOCK), 0)
    q_t = i * BLOCK + lax.broadcasted_iota(jnp.int32, (BLOCK, BLOCK), 1)

    def blocks_of(t):
        return [grp * t + g for g in range(grp)]

    def rows_of(j):
        return pl.ds(pl.multiple_of(jnp.minimum(j, nb - 1) * BLOCK, BLOCK), BLOCK)

    def col_sum(x):
        return jnp.sum(x, axis=0, keepdims=True)

    def fold8(x, op):
        return op(x.reshape(BLOCK // 8, 8, BLOCK), axis=0)

    def row_any(flags):
        return functools.reduce(jnp.maximum, [jnp.max(f) for f in flags]) > 0.0

    iq = [iq_ref[:, bb].reshape(hw, IDX_DIM) for bb in bbs]
    q4 = [qlat_ref[:, bb].reshape(hw, KV_RANK) for bb in bbs]
    wrow = []
    for bb in bbs:
        wq_t = ikw_ref[bb, pl.ds(pl.multiple_of(i * BLOCK, BLOCK), BLOCK), :].T
        wrow.append([wq_t[IDX_DIM + hh:IDX_DIM + hh + 1, :] * (IDX_DIM ** -0.5 * IDX_HEADS ** -0.5)
                     for hh in range(IDX_HEADS)])
    nt = (((1,), (1,)), ((), ()))

    def score_body(t, carry):
        mn, mx = list(carry[0]), list(carry[1])
        js = blocks_of(t)
        work = [(bb, j) for bb in bbs for j in js]
        iks = [ikw_ref[bb, rows_of(j), :][:, :IDX_DIM].astype(_BF) for bb, j in work]
        ckvs = [ckv_ref[bb, rows_of(j), :] for bb, j in work]
        raws = [lax.dot_general(ik, iq[bb], nt, preferred_element_type=_F32) for ik, (bb, j) in zip(iks, work)]
        lts = [lax.dot_general(ckv, q4[bb], nt, preferred_element_type=_F32) for ckv, (bb, j) in zip(ckvs, work)]
        outs = []
        for (bb, j), raw in zip(work, raws):
            s = jnp.zeros((BLOCK, BLOCK), _F32)
            for hh in range(IDX_HEADS):
                s = s + jnp.maximum(raw[:, hh * BLOCK:(hh + 1) * BLOCK], 0.0) * wrow[bb][hh]
            s = jnp.where(s == 0.0, 0.0, s)
            kpos = j * BLOCK + key_s
            adm = (kpos <= q_t) & (kpos >= PAD)
            s_lo = jnp.where(adm, s, -jnp.inf)
            mn[bb] = jnp.minimum(mn[bb], fold8(jnp.where(adm, s, jnp.inf), jnp.min))
            mx[bb] = jnp.maximum(mx[bb], fold8(s_lo, jnp.max))
            outs.append(s_lo)
        for (bb, j), s_lo, lt in zip(work, outs, lts):
            score_ref[bb, j] = s_lo
            logit_ref[bb, j] = lt
        return mn, mx

    mn, mx = lax.fori_loop(0, ngrp, score_body, ([jnp.full((8, BLOCK), jnp.inf, _F32)] * nbt,
                                                 [jnp.full((8, BLOCK), -jnp.inf, _F32)] * nbt))
    s_min = [jnp.min(x, axis=0, keepdims=True) for x in mn]
    s_max = [jnp.max(x, axis=0, keepdims=True) for x in mx]

    n_adm = (i * BLOCK + lax.broadcasted_iota(jnp.int32, (1, BLOCK), 1) - (PAD - 1)).astype(_F32)
    take_all = n_adm <= kf

    def count_where(pred):
        def body(t, acc):
            acc = list(acc)
            for bb in bbs:
                for j in blocks_of(t):
                    acc[bb] = acc[bb] + jnp.where(pred(bb, score_ref[bb, j]), 1.0, 0.0)
            return acc
        acc = lax.fori_loop(0, ngrp, body, [jnp.zeros((BLOCK, BLOCK), _F32)] * nbt)
        return [col_sum(a) for a in acc]

    def max_where(pred):
        def body(t, m):
            m = list(m)
            for bb in bbs:
                for j in blocks_of(t):
                    s = score_ref[bb, j]
                    m[bb] = jnp.maximum(m[bb], jnp.where(pred(bb, s), s, -jnp.inf))
            return m
        m = lax.fori_loop(0, ngrp, body, [jnp.full((BLOCK, BLOCK), -jnp.inf, _F32)] * nbt)
        return [jnp.max(x, axis=0, keepdims=True) for x in m]

    def select_threshold():
        def bis(_, st):
            lo, hi, c_hi = st
            mid = [(lo[bb] + hi[bb]) * 0.5 for bb in bbs]
            c = count_where(lambda bb, s: s > mid[bb])
            take = [c[bb] < kf for bb in bbs]
            return ([jnp.where(take[bb], lo[bb], mid[bb]) for bb in bbs],
                    [jnp.where(take[bb], mid[bb], hi[bb]) for bb in bbs],
                    [jnp.where(take[bb], c[bb], c_hi[bb]) for bb in bbs])

        _, hi_f, c_gt = lax.fori_loop(0, BISECT_ITERS, bis, (s_min, s_max, [jnp.zeros((1, BLOCK), _F32)] * nbt))
        thr = max_where(lambda bb, s: s <= hi_f[bb])
        g = count_where(lambda bb, s: s == thr[bb])

        def not_done(st):
            thr, c_gt, g, it = st
            pending = [jnp.where(take_all, 0.0, jnp.where(c_gt[bb] + g[bb] >= kf, 0.0, 1.0)) for bb in bbs]
            return jnp.logical_and(row_any(pending), it < 4096)

        def peel(st):
            thr, c_gt, g, it = st
            done = [jnp.logical_or(take_all, c_gt[bb] + g[bb] >= kf) for bb in bbs]
            thr_n = max_where(lambda bb, s: s < thr[bb])
            g_n = count_where(lambda bb, s: s == thr_n[bb])
            return ([jnp.where(done[bb], thr[bb], thr_n[bb]) for bb in bbs],
                    [jnp.where(done[bb], c_gt[bb], c_gt[bb] + g[bb]) for bb in bbs],
                    [jnp.where(done[bb], g[bb], g_n[bb]) for bb in bbs], it + 1)

        thr, c_gt, g, _ = lax.while_loop(not_done, peel, (thr, c_gt, g, jnp.int32(0)))
        return thr, c_gt, g

    zrow = jnp.zeros((1, BLOCK), _F32)
    thr, c_gt, g = lax.cond(i * BLOCK + (BLOCK - 1) - (PAD - 1) > k_top, select_threshold,
                            lambda: ([zrow] * nbt, [zrow] * nbt, [zrow] * nbt))
    thr = [jnp.where(take_all, -jnp.inf, x) for x in thr]
    need = [kf - x for x in c_gt]
    surplus_ties = row_any([jnp.where(take_all, 0.0, jnp.where(c_gt[bb] + g[bb] > kf, 1.0, 0.0)) for bb in bbs])

    def mask_pass(rank_ties):
        def body(t, st):
            eq_before, m8 = list(st[0]), [list(x) for x in st[1]]
            js = blocks_of(t)
            work = [(bb, j) for bb in bbs for j in js]
            ss = [score_ref[bb, j] for bb, j in work]
            lts = [logit_ref[bb, j] for bb, j in work]
            biases = [bias_ref[jnp.clip(i - j, 0, 2)] for j in js]
            blks = []
            for n, (bb, j) in enumerate(work):
                s, lt, bias = ss[n], lts[n], biases[n % grp]
                if rank_ties:
                    eq = (s == thr[bb]) & (s > -jnp.inf)
                    eqf = jnp.where(eq, 1.0, 0.0)
                    rank = eq_before[bb] + jnp.dot(early_ref[...], eqf.astype(_BF), preferred_element_type=_F32)
                    sel = (s > thr[bb]) | (eq & (rank < need[bb]))
                    eq_before[bb] = eq_before[bb] + col_sum(eqf)
                else:
                    sel = (s >= thr[bb]) & (s > -jnp.inf)
                row = []
                for hh in range(A_HEADS):
                    hs = slice(hh * BLOCK, (hh + 1) * BLOCK)
                    blk = jnp.where(sel, lt[:, hs] + bias[:, hs], NEG_INF)
                    m8[bb][hh] = jnp.maximum(m8[bb][hh], fold8(blk, jnp.max))
                    row.append(blk)
                blks.append(row)
            for (bb, j), row in zip(work, blks):
                for hh in range(A_HEADS):
                    logit_ref[bb, j, :, hh * BLOCK:(hh + 1) * BLOCK] = row[hh]
            return eq_before, m8

        _, m8 = lax.fori_loop(0, ngrp, body,
                              ([zrow] * nbt, [[jnp.full((8, BLOCK), NEG_INF, _F32)] * A_HEADS] * nbt))
        return m8

    m8 = lax.cond(surplus_ties, lambda: mask_pass(True), lambda: mask_pass(False))
    m = [jnp.concatenate([jnp.max(x, axis=0, keepdims=True) for x in m8[bb]], axis=1) for bb in bbs]

    p_ref[1] = jnp.zeros(p_ref.shape[1:], _BF)
    acc_ref[...] = jnp.zeros_like(acc_ref)

    def pv_body(t, l8):
        slot = lax.rem(t, 2)
        l8 = [list(x) for x in l8]
        accs = [acc_ref[bb] + jnp.dot(ckvt_ref[bb, jnp.maximum(t - 1, 0)], p_ref[1 - slot, bb],
                                      preferred_element_type=_F32) for bb in bbs]
        ps = [[jnp.exp(logit_ref[bb, j] - m[bb]) for j in blocks_of(t)] for bb in bbs]
        for bb in bbs:
            for hh in range(A_HEADS):
                l8[bb][hh] = l8[bb][hh] + sum(fold8(p[:, hh * BLOCK:(hh + 1) * BLOCK], jnp.sum) for p in ps[bb])
        for bb in bbs:
            acc_ref[bb] = accs[bb]
            p_ref[slot, bb] = jnp.concatenate([p.astype(_BF) for p in ps[bb]], axis=0)
        return l8

    l8 = lax.fori_loop(0, ngrp, pv_body, [[jnp.zeros((8, BLOCK), _F32)] * A_HEADS] * nbt)
    for bb in bbs:
        acc = acc_ref[bb] + jnp.dot(ckvt_ref[bb, ngrp - 1], p_ref[lax.rem(ngrp - 1, 2), bb],
                                    preferred_element_type=_F32)
        l = jnp.concatenate([col_sum(x) for x in l8[bb]], axis=1)
        o_lat = (acc / l).astype(_BF)
        out_t = jnp.zeros((A_WIDTH, BLOCK), _F32)
        for hh in range(A_HEADS):
            out_t = out_t + jnp.dot(wuvt_ref[hh], o_lat[:, hh * BLOCK:(hh + 1) * BLOCK], preferred_element_type=_F32)
        ya_ref[bb] = out_t.T.astype(_BF)


def _dsa(k_top, rel_bias, qlat4, iq4, ikw3, ckv3, ckvt4, bucket_t, wuvt_pad, early):
    bsz, plen, _ = ckv3.shape
    nb = plen // BLOCK
    nbt = STEP_BATCH
    ngmax = (nb + DSA_GROUP - 1) // DSA_GROUP
    nblk = ngmax * DSA_GROUP
    gk = DSA_GROUP * BLOCK
    hw = A_HEADS * BLOCK
    assert ckvt4.shape == (bsz, ngmax, KV_RANK, gk) and bsz % nbt == 0
    return pl.pallas_call(
        functools.partial(_dsa_kernel, k_top, nb),
        grid=(bsz // nbt, nb),
        in_specs=[
            pl.BlockSpec(memory_space=pltpu.SMEM),
            pl.BlockSpec((A_HEADS, nbt, BLOCK, KV_RANK), lambda b, i: (0, b, i, 0)),
            pl.BlockSpec((IDX_HEADS, nbt, BLOCK, IDX_DIM), lambda b, i: (0, b, i, 0)),
            pl.BlockSpec((nbt, plen, 128), lambda b, i: (b, 0, 0)),
            pl.BlockSpec((nbt, plen, KV_RANK), lambda b, i: (b, 0, 0)),
            pl.BlockSpec((nbt, ngmax, KV_RANK, gk), lambda b, i: (b, 0, 0, 0)),
            _const_spec((2, BLOCK, BLOCK)),
            _const_spec((A_HEADS, A_WIDTH, KV_RANK)),
            _const_spec((BLOCK, BLOCK)),
        ],
        out_specs=pl.BlockSpec((nbt, BLOCK, A_WIDTH), lambda b, i: (b, i, 0)),
        out_shape=jax.ShapeDtypeStruct((bsz, plen, A_WIDTH), _BF),
        scratch_shapes=[pltpu.VMEM((nbt, nblk, BLOCK, BLOCK), _F32),
                        pltpu.VMEM((nbt, nblk, BLOCK, hw), _F32),
                        pltpu.VMEM((2, nbt, gk, hw), _BF),
                        pltpu.VMEM((nbt, KV_RANK, hw), _F32),
                        pltpu.VMEM((3, BLOCK, hw), _F32)],
        compiler_params=pltpu.CompilerParams(dimension_semantics=("arbitrary", "arbitrary"),
                                             vmem_limit_bytes=VMEM_LIMIT),
        name="dsa",
    )(rel_bias, qlat4.reshape(A_HEADS, bsz, plen, KV_RANK), iq4.reshape(IDX_HEADS, bsz, plen, IDX_DIM),
      ikw3, ckv3, ckvt4, bucket_t, wuvt_pad, early).reshape(bsz * plen, A_WIDTH)


def _stick_kernel(nb, q_ref, k_ref, v_ref, low_ref, yc_ref, z_ref, w_ref, carry_ref, acc_ref):
    grp = STICK_GROUP
    nbt = STEP_BATCH
    i = pl.program_id(1)
    ngrp = (i + grp) // grp
    row_t = i * BLOCK + lax.broadcasted_iota(jnp.int32, (BLOCK, BLOCK), 0)
    lane = lax.broadcasted_iota(jnp.int32, (BLOCK, BLOCK), 1)
    first_head = lane < HEAD_DIM
    seqs = [(bb, hh) for bb in range(nbt) for hh in range(C_HEADS)]
    nseq = len(seqs)

    qm = []
    for bb, hh in seqs:
        qp = q_ref[bb, :, (hh // 2) * BLOCK:(hh // 2 + 1) * BLOCK].astype(_F32)
        keep = first_head if hh % 2 == 0 else jnp.logical_not(first_head)
        qm.append(jnp.where(keep, qp, 0.0).astype(_BF))

    def blocks_of(t):
        base = (ngrp - 1 - t) * grp
        return [base + (grp - 1 - g) for g in range(grp)]

    def rows_of(j):
        return pl.ds(pl.multiple_of(jnp.clip(j, 0, nb - 1) * BLOCK, BLOCK), BLOCK)

    def pair_tiles(ref, j):
        return [[ref[bb, rows_of(j), pr * BLOCK:(pr + 1) * BLOCK] for pr in range(C_HEADS // 2)]
                for bb in range(nbt)]

    def qk(t):
        out = []
        for j in blocks_of(t):
            kps = pair_tiles(k_ref, j)
            out += [lax.dot_general(qm[n], kps[bb][hh // 2], (((1,), (1,)), ((), ())),
                                    preferred_element_type=_F32) for n, (bb, hh) in enumerate(seqs)]
        return out

    def wv(t, slot, acc):
        vs = [pair_tiles(v_ref, j) for j in blocks_of(t)]
        new = []
        for n, (bb, hh) in enumerate(seqs):
            wcat = jnp.concatenate([w_ref[slot, g * nseq + n] for g in range(grp)], axis=1)
            vcat = jnp.concatenate([vs[g][bb][hh // 2] for g in range(grp)], axis=0)
            new.append(acc[n] + jnp.dot(wcat, vcat, preferred_element_type=_F32))
        return new

    for n, z in enumerate(qk(0)):
        z_ref[0, n] = z
    w_ref[1] = jnp.zeros(w_ref.shape[1:], _BF)
    carry_ref[...] = jnp.zeros_like(carry_ref)
    acc_ref[...] = jnp.zeros_like(acc_ref)

    def body(t, _):
        slot = lax.rem(t, 2)
        acc = wv(t - 1, 1 - slot, [acc_ref[n] for n in range(nseq)])
        z_next = qk(t + 1)
        carry = [carry_ref[n] for n in range(nseq)]
        zs = [z_ref[slot, n] for n in range(grp * nseq)]

        lks, zzs, masks = [], [], []
        for g, j in enumerate(blocks_of(t)):
            kpos = j * BLOCK + lane
            strict = (kpos < row_t) & (kpos >= PAD)
            masks.append(strict)
            for n in range(nseq):
                z = zs[g * nseq + n]
                zz = jnp.minimum(z, 0.0) - jnp.log(1.0 + jnp.exp(-jnp.abs(z)))
                lks.append(jnp.where(strict, zz - z, 0.0).astype(_BF))
                zzs.append(zz)
        cums = [jnp.dot(lk, low_ref[...], preferred_element_type=_F32) for lk in lks]
        ws = []
        for g in range(grp):
            for n in range(nseq):
                c = g * nseq + n
                w = jnp.where(masks[g], jnp.exp(zzs[c] + cums[c][:, :BLOCK] + carry[n]), 0.0)
                ws.append(w.astype(_BF))
                carry[n] = carry[n] + cums[c][:, BLOCK:]

        for n in range(nseq):
            acc_ref[n] = acc[n]
            carry_ref[n] = carry[n]
        for c in range(grp * nseq):
            z_ref[1 - slot, c] = z_next[c]
            w_ref[slot, c] = ws[c]
        return 0

    lax.fori_loop(0, ngrp, body, 0)
    acc = wv(ngrp - 1, lax.rem(ngrp - 1, 2), [acc_ref[n] for n in range(nseq)])
    for bb in range(nbt):
        yc_ref[bb] = jnp.concatenate([jnp.where(first_head, acc[bb * C_HEADS + 2 * pr], acc[bb * C_HEADS + 2 * pr + 1])
                                      for pr in range(C_HEADS // 2)], axis=1).astype(_BF)


def _stick(qc3, kc3, vc3, low_ext):
    bsz, plen, _ = kc3.shape
    nb = plen // BLOCK
    nbt = STEP_BATCH
    nseq = nbt * C_HEADS
    nchain = STICK_GROUP * nseq
    seq_blk = lambda rows: pl.BlockSpec((nbt, rows, C_WIDTH), (lambda b, i: (b, i, 0)) if rows == BLOCK
                                        else (lambda b, i: (b, 0, 0)))
    return pl.pallas_call(
        functools.partial(_stick_kernel, nb),
        grid=(bsz // nbt, nb),
        in_specs=[seq_blk(BLOCK), seq_blk(plen), seq_blk(plen), _const_spec((BLOCK, 2 * BLOCK))],
        out_specs=seq_blk(BLOCK),
        out_shape=jax.ShapeDtypeStruct((bsz, plen, C_WIDTH), _BF),
        scratch_shapes=[pltpu.VMEM((2, nchain, BLOCK, BLOCK), _F32),
                        pltpu.VMEM((2, nchain, BLOCK, BLOCK), _BF),
                        pltpu.VMEM((nseq, BLOCK, BLOCK), _F32),
                        pltpu.VMEM((nseq, BLOCK, BLOCK), _F32)],
        compiler_params=pltpu.CompilerParams(dimension_semantics=("arbitrary", "arbitrary"),
                                             vmem_limit_bytes=VMEM_LIMIT),
        name="stick",
    )(qc3, kc3, vc3, low_ext).reshape(bsz * plen, C_WIDTH)


def _s5_kernel(bsz, u_ref, wbu_ref, a_ref, wc_ref, d_ref, wglu_ref, y_ref, bur_ref, bui_ref, st_ref):
    c = pl.program_id(0)
    slab_w = S5_LANES // S5_SLABS
    ch_w = B_WIDTH // S5_SLABS

    @pl.when(c == 0)
    def _():
        st_ref[...] = jnp.zeros_like(st_ref)

    u = u_ref[...]
    for k in range(S5_SLABS):
        o = jnp.dot(u[:, k * ch_w:(k + 1) * ch_w], wbu_ref[k], preferred_element_type=_F32)
        bur_ref[:, k * slab_w:(k + 1) * slab_w] = o[:, :slab_w]
        bui_ref[:, k * slab_w:(k + 1) * slab_w] = o[:, slab_w:]

    for k in range(S5_SLABS):
        ls = slice(k * slab_w, (k + 1) * slab_w)
        ar = jnp.broadcast_to(a_ref[0:1, ls], (bsz, slab_w))
        ai = jnp.broadcast_to(a_ref[1:2, ls], (bsz, slab_w))

        def step(t, st, ls=ls, ar=ar, ai=ai):
            hr, hi = st
            rows = pl.ds(pl.multiple_of(t * bsz, bsz), bsz)
            nhr = ar * hr - ai * hi + bur_ref[rows, ls]
            nhi = ar * hi + ai * hr + bui_ref[rows, ls]
            bur_ref[rows, ls] = nhr
            bui_ref[rows, ls] = nhi
            return nhr, nhi

        hr, hi = lax.fori_loop(0, S5_CHUNK, step, (st_ref[0, :, ls], st_ref[1, :, ls]), unroll=4)
        st_ref[0, :, ls] = hr
        st_ref[1, :, ls] = hi

    ys = []
    for k in range(S5_SLABS):
        ls = slice(k * slab_w, (k + 1) * slab_w)
        hcat = jnp.concatenate([bur_ref[:, ls].astype(_BF), bui_ref[:, ls].astype(_BF)], axis=1)
        ys.append(jnp.dot(hcat, wc_ref[k], preferred_element_type=_F32))
    y = jnp.concatenate(ys, axis=1) + d_ref[...] * u.astype(_F32)
    y = 0.5 * y * (1.0 + jnp.tanh(math.sqrt(2.0 / math.pi) * (y + 0.044715 * (y * y * y))))
    gate = jnp.dot(y.astype(_BF), wglu_ref[...], preferred_element_type=_F32)
    y_ref[...] = (y * (1.0 / (1.0 + jnp.exp(-gate)))).astype(_BF)


def _s5(bsz, u_tb, wbu, a_pack, wc, d_skip, w_glu):
    rows = u_tb.shape[0]
    rc = S5_CHUNK * bsz
    return pl.pallas_call(
        functools.partial(_s5_kernel, bsz),
        grid=(rows // rc,),
        in_specs=[
            pl.BlockSpec((rc, B_WIDTH), lambda c: (c, 0)),
            _const_spec((S5_SLABS, B_WIDTH // S5_SLABS, 2 * S5_LANES // S5_SLABS)),
            _const_spec((2, S5_LANES)),
            _const_spec((S5_SLABS, 2 * S5_LANES // S5_SLABS, B_WIDTH // S5_SLABS)),
            _const_spec((1, B_WIDTH)),
            _const_spec((B_WIDTH, B_WIDTH)),
        ],
        out_specs=pl.BlockSpec((rc, B_WIDTH), lambda c: (c, 0)),
        out_shape=jax.ShapeDtypeStruct((rows, B_WIDTH), _BF),
        scratch_shapes=[pltpu.VMEM((rc, S5_LANES), _F32), pltpu.VMEM((rc, S5_LANES), _F32),
                        pltpu.VMEM((2, bsz, S5_LANES), _F32)],
        compiler_params=pltpu.CompilerParams(dimension_semantics=("arbitrary",),
                                             vmem_limit_bytes=VMEM_LIMIT),
        name="s5",
    )(u_tb, wbu, a_pack, wc, d_skip, w_glu)


def _mix_mlp_kernel(plen, final, h_ref, ya_ref, yb_ref, yc_ref, ga_ref, gb_ref, gc_ref, wo_ref,
                    g2_ref, w1_ref, w2_ref, gf_ref, o_ref):
    tm = h_ref.shape[0]
    y = jnp.concatenate([_rms(ya_ref[...].astype(_F32), ga_ref[...]),
                         _rms(yb_ref[...].astype(_F32), gb_ref[...]),
                         _rms(yc_ref[...].astype(_F32), gc_ref[...])], axis=1).astype(_BF)
    h = h_ref[...] + jnp.dot(y, wo_ref[...], preferred_element_type=_F32)
    hn = _rms(h, g2_ref[...]).astype(_BF)
    ffc = D_FF // 4
    for cc in range(4):
        a = jnp.dot(hn, w1_ref[:, cc * ffc:(cc + 1) * ffc], preferred_element_type=_F32)
        a = jnp.square(jnp.maximum(a, 0.0)).astype(_BF)
        h = h + jnp.dot(a, w2_ref[cc * ffc:(cc + 1) * ffc, :], preferred_element_type=_F32)
    if final:
        h = _rms(h, gf_ref[...])
    r = (pl.program_id(0) * tm + lax.broadcasted_iota(jnp.int32, (tm, 1), 0)).astype(_F32)
    pos = r - jnp.floor((r + 0.5) * (1.0 / plen)) * plen
    o_ref[...] = jnp.where(pos >= PAD, h, 0.0)


def _mix_mlp(plen, final, h2d, ya, yb, yc, ga, gb, gc, wo, g2, w1, w2, gf):
    rows = h2d.shape[0]
    tm = ROW_TILE
    row = lambda w: pl.BlockSpec((tm, w), lambda i: (i, 0))
    return pl.pallas_call(
        functools.partial(_mix_mlp_kernel, plen, final),
        grid=(rows // tm,),
        in_specs=[row(D_MODEL), row(A_WIDTH), row(B_WIDTH), row(C_WIDTH),
                  _const_spec((1, A_WIDTH)), _const_spec((1, B_WIDTH)), _const_spec((1, C_WIDTH)),
                  _const_spec((D_MODEL, D_MODEL)), _const_spec((1, D_MODEL)),
                  _const_spec((D_MODEL, D_FF)), _const_spec((D_FF, D_MODEL)), _const_spec((1, D_MODEL))],
        out_specs=row(D_MODEL),
        out_shape=jax.ShapeDtypeStruct((rows, D_MODEL), _F32),
        compiler_params=pltpu.CompilerParams(dimension_semantics=("arbitrary",),
                                             vmem_limit_bytes=VMEM_LIMIT),
        name="mix_mlp",
    )(h2d, ya, yb, yc, ga, gb, gc, wo, g2, w1, w2, gf)


def _t5_bucket_tiles():
    max_exact = NUM_BUCKETS // 2
    r = jnp.arange(BLOCK)[:, None]
    c = jnp.arange(BLOCK)[None, :]
    tiles = []
    for dd in range(2):
        n = jnp.maximum(dd * BLOCK + r - c, 0)
        nf = jnp.maximum(n, max_exact).astype(_F32)
        large = max_exact + (jnp.log(nf / max_exact) / math.log(MAX_DISTANCE / max_exact)
                             * (NUM_BUCKETS - max_exact)).astype(jnp.int32)
        large = jnp.minimum(large, NUM_BUCKETS - 1)
        tiles.append(jnp.where(n < max_exact, n, large))
    return jnp.stack(tiles).astype(jnp.int32)


def _pack_w_in(w_in):
    sizes = (A_WIDTH, KV_RANK, IDX_HEADS * IDX_DIM, IDX_DIM, IDX_HEADS, B_WIDTH, C_WIDTH, C_WIDTH, C_WIDTH)
    offs = np.cumsum((0,) + sizes)
    qa, ckv, iq, ik, iw, ub, qc, kc, vc = [w_in[:, offs[n]:offs[n + 1]] for n in range(len(sizes))]
    zeros = lambda n: jnp.zeros((w_in.shape[0], n), w_in.dtype)
    cols = [qa, ckv]
    for hh in range(IDX_HEADS):
        cols += [iq[:, hh * IDX_DIM:(hh + 1) * IDX_DIM], zeros(128 - IDX_DIM)]
    cols += [ik, iw, zeros(128 - IDX_DIM - IDX_HEADS), ub, qc, kc, vc]
    return jnp.concatenate(cols, axis=1).astype(_BF)


def _block_diag(blocks):
    n, r, c = blocks.shape
    eye = jnp.eye(n, dtype=blocks.dtype)
    return (eye[:, None, :, None] * blocks[:, :, None, :]).reshape(n * r, n * c)


def _s5_params(lam_re, lam_im, log_dt, b_re, b_im, c_re, c_im):
    dt = jnp.exp(log_dt)[:, None]
    mag = jnp.exp(lam_re * dt)
    ar = mag * jnp.cos(lam_im * dt)
    ai = mag * jnp.sin(lam_im * dt)
    den = lam_re * lam_re + lam_im * lam_im
    nr, ni = ar - 1.0, ai
    fr = (nr * lam_re + ni * lam_im) / den
    fi = (ni * lam_re - nr * lam_im) / den
    bfr = fr[:, :, None] * b_re - fi[:, :, None] * b_im
    bfi = fr[:, :, None] * b_im + fi[:, :, None] * b_re
    gps = S5_GROUPS // S5_SLABS
    wbu, wc = [], []
    for k in range(S5_SLABS):
        gs = slice(k * gps, (k + 1) * gps)
        wbu.append(jnp.concatenate([_block_diag(jnp.swapaxes(bfr[gs], 1, 2)),
                                    _block_diag(jnp.swapaxes(bfi[gs], 1, 2))], axis=1))
        wc.append(jnp.concatenate([_block_diag(jnp.swapaxes(c_re[gs], 1, 2)),
                                   _block_diag(jnp.swapaxes(-c_im[gs], 1, 2))], axis=0))
    a_pack = jnp.stack([ar.reshape(-1), ai.reshape(-1)])
    return jnp.stack(wbu).astype(_BF), a_pack, jnp.stack(wc).astype(_BF)


def kernel(x, meta_tokens, rel_bias, norm1_g, w_in, kv_norm_g, w_uk, w_uv, lambda_re, lambda_im, log_dt,
           b_re, b_im, c_re, c_im, d_skip, w_glu, gn_a, gn_b, gn_c, w_out, norm2_g, w_mlp1, w_mlp2, final_g):
    bsz, seq, _ = x.shape
    depth = w_in.shape[0]
    n_keys = seq + N_META
    k_top = min(TOPK_MAX, n_keys // 4)
    plen = n_keys + PAD
    rows = bsz * plen
    assert seq % BLOCK == 0 and rows % ROW_TILE == 0 and plen % S5_CHUNK == 0 and bsz % 8 == 0
    assert bsz % STEP_BATCH == 0

    head = jnp.concatenate([jnp.zeros((PAD, D_MODEL), x.dtype), meta_tokens.astype(x.dtype)], axis=0)
    h = jnp.concatenate([jnp.broadcast_to(head[None], (bsz, BLOCK, D_MODEL)), x], axis=1).reshape(rows, D_MODEL)

    bucket_t = jnp.swapaxes(_t5_bucket_tiles(), 1, 2)
    ar_ = jnp.arange(BLOCK)
    lower = (ar_[:, None] > ar_[None, :]).astype(_BF)
    ones = jnp.ones((BLOCK, BLOCK), _BF)
    low_ext = jnp.concatenate([lower, ones], axis=1)
    row1 = lambda v: v.reshape(1, -1).astype(_F32)

    for l in range(depth):
        w_all = _pack_w_in(w_in[l])
        wuk_bd = _block_diag(jnp.transpose(w_uk[l], (1, 2, 0))).astype(_BF)
        wuvt_pad = jnp.stack([jnp.pad(w_uv[l][:, hh, :].T, ((hh * HEAD_DIM, A_WIDTH - (hh + 1) * HEAD_DIM), (0, 0)))
                              for hh in range(A_HEADS)]).astype(_BF)
        qlat4, ckv, iq4, ikw, ub, qc, kc, vc = _inproj(h, row1(norm1_g[l]), w_all, row1(kv_norm_g[l]), wuk_bd)

        ckv3 = ckv.reshape(bsz, plen, KV_RANK)
        gk = DSA_GROUP * BLOCK
        ckvt3 = jnp.swapaxes(jnp.pad(ckv3, ((0, 0), (0, -plen % gk), (0, 0))).reshape(bsz, -1, gk, KV_RANK), 2, 3)
        ya = _dsa(k_top, rel_bias.astype(_F32), qlat4, iq4, ikw.reshape(bsz, plen, 128),
                  ckv3, ckvt3, bucket_t, wuvt_pad, lower)

        wbu, a_pack, wc = _s5_params(lambda_re[l], lambda_im[l], log_dt[l], b_re[l], b_im[l], c_re[l], c_im[l])
        u_tb = jnp.swapaxes(ub.reshape(bsz, plen, B_WIDTH), 0, 1).reshape(rows, B_WIDTH)
        yb_tb = _s5(bsz, u_tb, wbu, a_pack, wc, row1(d_skip[l]), w_glu[l].astype(_BF))
        yb = jnp.swapaxes(yb_tb.reshape(plen, bsz, B_WIDTH), 0, 1).reshape(rows, B_WIDTH)

        yc = _stick(qc.reshape(bsz, plen, C_WIDTH), kc.reshape(bsz, plen, C_WIDTH),
                    vc.reshape(bsz, plen, C_WIDTH), low_ext)

        h = _mix_mlp(plen, l == depth - 1, h, ya, yb, yc, row1(gn_a[l]), row1(gn_b[l]), row1(gn_c[l]),
                     w_out[l].astype(_BF), row1(norm2_g[l]), w_mlp1[l].astype(_BF), w_mlp2[l].astype(_BF),
                     row1(final_g))

    return h.reshape(bsz, plen, D_MODEL)[:, BLOCK:]
```

```python
import functools
import math

import jax
import jax.numpy as jnp
import numpy as np
from jax import lax
from jax.experimental import pallas as pl
from jax.experimental.pallas import tpu as pltpu

D_MODEL = 1024
N_META = 16
BLOCK = 128
PAD = BLOCK - N_META
HEAD_DIM = 64
A_WIDTH = 256
A_HEADS = 4
KV_RANK = 128
IDX_HEADS = 4
IDX_DIM = 64
TOPK_MAX = 256
B_WIDTH = 512
S5_GROUP = 16
S5_GROUPS = 32
S5_STATE = 64
C_WIDTH = 256
C_HEADS = 4
D_FF = 4 * D_MODEL
NUM_BUCKETS = 32
MAX_DISTANCE = 128
RMS_EPS = 1e-6
NEG_INF = -1e30

ROW_TILE = 512
S5_CHUNK = 32
S5_LANES = S5_GROUPS * S5_STATE
S5_SLABS = 4
STEP_BATCH = 4
DSA_GROUP = 1
BISECT_ITERS = 18
STICK_GROUP = 1
VMEM_LIMIT = 56 * 1024 * 1024

_BF = jnp.bfloat16
_F32 = jnp.float32

_C_QA = 0
_C_CKV = _C_QA + A_WIDTH
_C_IQ = _C_CKV + KV_RANK
_C_IKW = _C_IQ + IDX_HEADS * 128
_C_UB = _C_IKW + 128
_C_QC = _C_UB + B_WIDTH
_C_KC = _C_QC + C_WIDTH
_C_VC = _C_KC + C_WIDTH
_N_PROJ = _C_VC + C_WIDTH


def _rms(x, g):
    return x * lax.rsqrt(jnp.mean(x * x, axis=-1, keepdims=True) + RMS_EPS) * g


def _inproj_kernel(h_ref, g1_ref, w_ref, kvg_ref, wuk_ref,
                   qlat_ref, ckv_ref, iq_ref, ikw_ref, ub_ref, qc_ref, kc_ref, vc_ref):
    hn = _rms(h_ref[...], g1_ref[...]).astype(_BF)
    proj = jnp.dot(hn, w_ref[...], preferred_element_type=_F32)
    qa = proj[:, _C_QA:_C_QA + A_WIDTH].astype(_BF)
    qlat = jnp.dot(qa, wuk_ref[...], preferred_element_type=_F32) * (HEAD_DIM ** -0.5)
    for hh in range(A_HEADS):
        qlat_ref[hh] = qlat[:, hh * KV_RANK:(hh + 1) * KV_RANK].astype(_BF)
    ckv_ref[...] = _rms(proj[:, _C_CKV:_C_CKV + KV_RANK], kvg_ref[...]).astype(_BF)
    for hh in range(IDX_HEADS):
        c0 = _C_IQ + hh * 128
        iq_ref[hh] = proj[:, c0:c0 + IDX_DIM].astype(_BF)
    ikw_ref[...] = proj[:, _C_IKW:_C_IKW + 128]
    ub_ref[...] = proj[:, _C_UB:_C_UB + B_WIDTH].astype(_BF)
    qc_ref[...] = (proj[:, _C_QC:_C_QC + C_WIDTH] * (HEAD_DIM ** -0.5)).astype(_BF)
    kc_ref[...] = proj[:, _C_KC:_C_KC + C_WIDTH].astype(_BF)
    vc_ref[...] = proj[:, _C_VC:_C_VC + C_WIDTH].astype(_BF)


def _const_spec(shape):
    nd = len(shape)
    return pl.BlockSpec(shape, lambda *_: (0,) * nd, pipeline_mode=pl.Buffered(1))


def _inproj(h2d, g1, w_all, kvg, wuk_bd):
    rows = h2d.shape[0]
    tm = ROW_TILE
    row = lambda w: pl.BlockSpec((tm, w), lambda i: (i, 0))
    head = lambda w: pl.BlockSpec((4, tm, w), lambda i: (0, i, 0))
    out_shape = (
        jax.ShapeDtypeStruct((A_HEADS, rows, KV_RANK), _BF),
        jax.ShapeDtypeStruct((rows, KV_RANK), _BF),
        jax.ShapeDtypeStruct((IDX_HEADS, rows, IDX_DIM), _BF),
        jax.ShapeDtypeStruct((rows, 128), _F32),
        jax.ShapeDtypeStruct((rows, B_WIDTH), _BF),
        jax.ShapeDtypeStruct((rows, C_WIDTH), _BF),
        jax.ShapeDtypeStruct((rows, C_WIDTH), _BF),
        jax.ShapeDtypeStruct((rows, C_WIDTH), _BF),
    )
    return pl.pallas_call(
        _inproj_kernel,
        grid=(rows // tm,),
        in_specs=[row(D_MODEL), _const_spec((1, D_MODEL)), _const_spec((D_MODEL, _N_PROJ)),
                  _const_spec((1, KV_RANK)), _const_spec((A_WIDTH, A_HEADS * KV_RANK))],
        out_specs=(head(KV_RANK), row(KV_RANK), head(IDX_DIM), row(128), row(B_WIDTH),
                   row(C_WIDTH), row(C_WIDTH), row(C_WIDTH)),
        out_shape=out_shape,
        compiler_params=pltpu.CompilerParams(dimension_semantics=("arbitrary",),
                                             vmem_limit_bytes=VMEM_LIMIT),
        name="inproj",
    )(h2d, g1, w_all, kvg, wuk_bd)


def _dsa_kernel(k_top, nb, relb_ref, qlat_ref, iq_ref, ikw_ref, ckv_ref, ckvt_ref, bucket_ref, wuvt_ref,
                early_ref, ya_ref, score_ref, logit_ref, p_ref, acc_ref, bias_ref):
    grp = DSA_GROUP
    nbt = STEP_BATCH
    i = pl.program_id(1)
    ngrp = (i + grp) // grp
    kf = float(k_top)
    hw = A_HEADS * BLOCK
    bbs = range(nbt)

    @pl.when(i == 0)
    def _():
        for dd in range(2):
            bk = bucket_ref[dd]
            for hh in range(A_HEADS):
                acc = jnp.zeros((BLOCK, BLOCK), _F32)
                for b_ in range(NUM_BUCKETS):
                    acc = jnp.where(bk == b_, relb_ref[b_, hh], acc)
                bias_ref[dd, :, hh * BLOCK:(hh + 1) * BLOCK] = acc
        for hh in range(A_HEADS):
            bias_ref[2, :, hh * BLOCK:(hh + 1) * BLOCK] = jnp.full((BLOCK, BLOCK), relb_ref[NUM_BUCKETS - 1, hh], _F32)

    key_s = lax.broadcasted_iota(jnp.int32, (BLOCK, BLOCK), 0)
    q_t = i * BLOCK + lax.broadcasted_iota(jnp.int32, (BLOCK, BLOCK), 1)

    def blocks_of(t):
        return [grp * t + g for g in range(grp)]

    def rows_of(j):
        return pl.ds(pl.multiple_of(jnp.minimum(j, nb - 1) * BLOCK, BLOCK), BLOCK)

    def col_sum(x):
        return jnp.sum(x, axis=0, keepdims=True)

    def fold8(x, op):
        return op(x.reshape(BLOCK // 8, 8, BLOCK), axis=0)

    def row_any(flags):
        return functools.reduce(jnp.maximum, [jnp.max(f) for f in flags]) > 0.0

    iq = [iq_ref[:, bb].reshape(hw, IDX_DIM) for bb in bbs]
    q4 = [qlat_ref[:, bb].reshape(hw, KV_RANK) for bb in bbs]
    wrow = []
    for bb in bbs:
        wq_t = ikw_ref[bb, pl.ds(pl.multiple_of(i * BLOCK, BLOCK), BLOCK), :].T
        wrow.append([wq_t[IDX_DIM + hh:IDX_DIM + hh + 1, :] * (IDX_DIM ** -0.5 * IDX_HEADS ** -0.5)
                     for hh in range(IDX_HEADS)])
    nt = (((1,), (1,)), ((), ()))

    def score_body(t, carry):
        mn, mx = list(carry[0]), list(carry[1])
        js = blocks_of(t)
        work = [(bb, j) for bb in bbs for j in js]
        iks = [ikw_ref[bb, rows_of(j), :][:, :IDX_DIM].astype(_BF) for bb, j in work]
        ckvs = [ckv_ref[bb, rows_of(j), :] for bb, j in work]
        raws = [lax.dot_general(ik, iq[bb], nt, preferred_element_type=_F32) for ik, (bb, j) in zip(iks, work)]
        lts = [lax.dot_general(ckv, q4[bb], nt, preferred_element_type=_F32) for ckv, (bb, j) in zip(ckvs, work)]
        outs = []
        for (bb, j), raw in zip(work, raws):
            s = jnp.zeros((BLOCK, BLOCK), _F32)
            for hh in range(IDX_HEADS):
                s = s + jnp.maximum(raw[:, hh * BLOCK:(hh + 1) * BLOCK], 0.0) * wrow[bb][hh]
            s = jnp.where(s == 0.0, 0.0, s)
            kpos = j * BLOCK + key_s
            adm = (kpos <= q_t) & (kpos >= PAD)
            s_lo = jnp.where(adm, s, -jnp.inf)
            mn[bb] = jnp.minimum(mn[bb], fold8(jnp.where(adm, s, jnp.inf), jnp.min))
            mx[bb] = jnp.maximum(mx[bb], fold8(s_lo, jnp.max))
            outs.append(s_lo)
        for (bb, j), s_lo, lt in zip(work, outs, lts):
            score_ref[bb, j] = s_lo
            logit_ref[bb, j] = lt
        return mn, mx

    mn, mx = lax.fori_loop(0, ngrp, score_body, ([jnp.full((8, BLOCK), jnp.inf, _F32)] * nbt,
                                                 [jnp.full((8, BLOCK), -jnp.inf, _F32)] * nbt))
    s_min = [jnp.min(x, axis=0, keepdims=True) for x in mn]
    s_max = [jnp.max(x, axis=0, keepdims=True) for x in mx]

    n_adm = (i * BLOCK + lax.broadcasted_iota(jnp.int32, (1, BLOCK), 1) - (PAD - 1)).astype(_F32)
    take_all = n_adm <= kf

    def count_where(pred):
        def body(t, acc):
            acc = list(acc)
            for bb in bbs:
                for j in blocks_of(t):
                    acc[bb] = acc[bb] + fold8(jnp.where(pred(bb, score_ref[bb, j]), 1.0, 0.0), jnp.sum)
            return acc
        acc = lax.fori_loop(0, ngrp, body, [jnp.zeros((8, BLOCK), _F32)] * nbt)
        return [col_sum(a) for a in acc]

    def max_where(pred):
        def body(t, m):
            m = list(m)
            for bb in bbs:
                for j in blocks_of(t):
                    s = score_ref[bb, j]
                    m[bb] = jnp.maximum(m[bb], fold8(jnp.where(pred(bb, s), s, -jnp.inf), jnp.max))
            return m
        m = lax.fori_loop(0, ngrp, body, [jnp.full((8, BLOCK), -jnp.inf, _F32)] * nbt)
        return [jnp.max(x, axis=0, keepdims=True) for x in m]

    def select_threshold():
        def bis(_, st):
            lo, hi, c_hi = st
            mid = [(lo[bb] + hi[bb]) * 0.5 for bb in bbs]
            c = count_where(lambda bb, s: s > mid[bb])
            take = [c[bb] < kf for bb in bbs]
            return ([jnp.where(take[bb], lo[bb], mid[bb]) for bb in bbs],
                    [jnp.where(take[bb], mid[bb], hi[bb]) for bb in bbs],
                    [jnp.where(take[bb], c[bb], c_hi[bb]) for bb in bbs])

        _, hi_f, c_gt = lax.fori_loop(0, BISECT_ITERS, bis, (s_min, s_max, [jnp.zeros((1, BLOCK), _F32)] * nbt))
        thr = max_where(lambda bb, s: s <= hi_f[bb])
        g = count_where(lambda bb, s: s == thr[bb])

        def not_done(st):
            thr, c_gt, g, it = st
            pending = [jnp.where(take_all, 0.0, jnp.where(c_gt[bb] + g[bb] >= kf, 0.0, 1.0)) for bb in bbs]
            return jnp.logical_and(row_any(pending), it < 4096)

        def peel(st):
            thr, c_gt, g, it = st
            done = [jnp.logical_or(take_all, c_gt[bb] + g[bb] >= kf) for bb in bbs]
            thr_n = max_where(lambda bb, s: s < thr[bb])
            g_n = count_where(lambda bb, s: s == thr_n[bb])
            return ([jnp.where(done[bb], thr[bb], thr_n[bb]) for bb in bbs],
                    [jnp.where(done[bb], c_gt[bb], c_gt[bb] + g[bb]) for bb in bbs],
                    [jnp.where(done[bb], g[bb], g_n[bb]) for bb in bbs], it + 1)

        thr, c_gt, g, _ = lax.while_loop(not_done, peel, (thr, c_gt, g, jnp.int32(0)))
        return thr, c_gt, g

    zrow = jnp.zeros((1, BLOCK), _F32)
    thr, c_gt, g = lax.cond(i * BLOCK + (BLOCK - 1) - (PAD - 1) > k_top, select_threshold,
                            lambda: ([zrow] * nbt, [zrow] * nbt, [zrow] * nbt))
    thr = [jnp.where(take_all, -jnp.inf, x) for x in thr]
    need = [kf - x for x in c_gt]
    surplus_ties = row_any([jnp.where(take_all, 0.0, jnp.where(c_gt[bb] + g[bb] > kf, 1.0, 0.0)) for bb in bbs])

    def mask_pass(rank_ties):
        def body(t, st):
            eq_before, m8 = list(st[0]), [list(x) for x in st[1]]
            js = blocks_of(t)
            work = [(bb, j) for bb in bbs for j in js]
            ss = [score_ref[bb, j] for bb, j in work]
            lts = [logit_ref[bb, j] for bb, j in work]
            biases = [bias_ref[jnp.clip(i - j, 0, 2)] for j in js]
            blks = []
            for n, (bb, j) in enumerate(work):
                s, lt, bias = ss[n], lts[n], biases[n % grp]
                if rank_ties:
                    eq = (s == thr[bb]) & (s > -jnp.inf)
                    eqf = jnp.where(eq, 1.0, 0.0)
                    rank = eq_before[bb] + jnp.dot(early_ref[...], eqf.astype(_BF), preferred_element_type=_F32)
                    sel = (s > thr[bb]) | (eq & (rank < need[bb]))
                    eq_before[bb] = eq_before[bb] + col_sum(eqf)
                else:
                    sel = (s >= thr[bb]) & (s > -jnp.inf)
                row = []
                for hh in range(A_HEADS):
                    hs = slice(hh * BLOCK, (hh + 1) * BLOCK)
                    blk = jnp.where(sel, lt[:, hs] + bias[:, hs], NEG_INF)
                    m8[bb][hh] = jnp.maximum(m8[bb][hh], fold8(blk, jnp.max))
                    row.append(blk)
                blks.append(row)
            for (bb, j), row in zip(work, blks):
                for hh in range(A_HEADS):
                    logit_ref[bb, j, :, hh * BLOCK:(hh + 1) * BLOCK] = row[hh]
            return eq_before, m8

        _, m8 = lax.fori_loop(0, ngrp, body,
                              ([zrow] * nbt, [[jnp.full((8, BLOCK), NEG_INF, _F32)] * A_HEADS] * nbt))
        return m8

    m8 = lax.cond(surplus_ties, lambda: mask_pass(True), lambda: mask_pass(False))
    m = [jnp.concatenate([jnp.max(x, axis=0, keepdims=True) for x in m8[bb]], axis=1) for bb in bbs]

    p_ref[1] = jnp.zeros(p_ref.shape[1:], _BF)
    acc_ref[...] = jnp.zeros_like(acc_ref)

    def pv_body(t, l8):
        slot = lax.rem(t, 2)
        l8 = [list(x) for x in l8]
        accs = [acc_ref[bb] + jnp.dot(ckvt_ref[bb, jnp.maximum(t - 1, 0)], p_ref[1 - slot, bb],
                                      preferred_element_type=_F32) for bb in bbs]
        ps = [[jnp.exp(logit_ref[bb, j] - m[bb]) for j in blocks_of(t)] for bb in bbs]
        for bb in bbs:
            for hh in range(A_HEADS):
                l8[bb][hh] = l8[bb][hh] + sum(fold8(p[:, hh * BLOCK:(hh + 1) * BLOCK], jnp.sum) for p in ps[bb])
        for bb in bbs:
            acc_ref[bb] = accs[bb]
            p_ref[slot, bb] = jnp.concatenate([p.astype(_BF) for p in ps[bb]], axis=0)
        return l8

    l8 = lax.fori_loop(0, ngrp, pv_body, [[jnp.zeros((8, BLOCK), _F32)] * A_HEADS] * nbt)
    for bb in bbs:
        acc = acc_ref[bb] + jnp.dot(ckvt_ref[bb, ngrp - 1], p_ref[lax.rem(ngrp - 1, 2), bb],
                                    preferred_element_type=_F32)
        l = jnp.concatenate([col_sum(x) for x in l8[bb]], axis=1)
        o_lat = (acc / l).astype(_BF)
        out_t = jnp.zeros((A_WIDTH, BLOCK), _F32)
        for hh in range(A_HEADS):
            out_t = out_t + jnp.dot(wuvt_ref[hh], o_lat[:, hh * BLOCK:(hh + 1) * BLOCK], preferred_element_type=_F32)
        ya_ref[bb] = out_t.T.astype(_BF)


def _dsa(k_top, rel_bias, qlat4, iq4, ikw3, ckv3, ckvt4, bucket_t, wuvt_pad, early):
    bsz, plen, _ = ckv3.shape
    nb = plen // BLOCK
    nbt = STEP_BATCH
    ngmax = (nb + DSA_GROUP - 1) // DSA_GROUP
    nblk = ngmax * DSA_GROUP
    gk = DSA_GROUP * BLOCK
    hw = A_HEADS * BLOCK
    assert ckvt4.shape == (bsz, ngmax, KV_RANK, gk) and bsz % nbt == 0
    return pl.pallas_call(
        functools.partial(_dsa_kernel, k_top, nb),
        grid=(bsz // nbt, nb),
        in_specs=[
            pl.BlockSpec(memory_space=pltpu.SMEM),
            pl.BlockSpec((A_HEADS, nbt, BLOCK, KV_RANK), lambda b, i: (0, b, i, 0)),
            pl.BlockSpec((IDX_HEADS, nbt, BLOCK, IDX_DIM), lambda b, i: (0, b, i, 0)),
            pl.BlockSpec((nbt, plen, 128), lambda b, i: (b, 0, 0)),
            pl.BlockSpec((nbt, plen, KV_RANK), lambda b, i: (b, 0, 0)),
            pl.BlockSpec((nbt, ngmax, KV_RANK, gk), lambda b, i: (b, 0, 0, 0)),
            _const_spec((2, BLOCK, BLOCK)),
            _const_spec((A_HEADS, A_WIDTH, KV_RANK)),
            _const_spec((BLOCK, BLOCK)),
        ],
        out_specs=pl.BlockSpec((nbt, BLOCK, A_WIDTH), lambda b, i: (b, i, 0)),
        out_shape=jax.ShapeDtypeStruct((bsz, plen, A_WIDTH), _BF),
        scratch_shapes=[pltpu.VMEM((nbt, nblk, BLOCK, BLOCK), _F32),
                        pltpu.VMEM((nbt, nblk, BLOCK, hw), _F32),
                        pltpu.VMEM((2, nbt, gk, hw), _BF),
                        pltpu.VMEM((nbt, KV_RANK, hw), _F32),
                        pltpu.VMEM((3, BLOCK, hw), _F32)],
        compiler_params=pltpu.CompilerParams(dimension_semantics=("arbitrary", "arbitrary"),
                                             vmem_limit_bytes=VMEM_LIMIT),
        name="dsa",
    )(rel_bias, qlat4.reshape(A_HEADS, bsz, plen, KV_RANK), iq4.reshape(IDX_HEADS, bsz, plen, IDX_DIM),
      ikw3, ckv3, ckvt4, bucket_t, wuvt_pad, early).reshape(bsz * plen, A_WIDTH)


def _stick_kernel(nb, q_ref, k_ref, v_ref, low_ref, yc_ref, z_ref, w_ref, carry_ref, acc_ref):
    grp = STICK_GROUP
    nbt = STEP_BATCH
    i = pl.program_id(1)
    ngrp = (i + grp) // grp
    row_t = i * BLOCK + lax.broadcasted_iota(jnp.int32, (BLOCK, BLOCK), 0)
    lane = lax.broadcasted_iota(jnp.int32, (BLOCK, BLOCK), 1)
    first_head = lane < HEAD_DIM
    seqs = [(bb, hh) for bb in range(nbt) for hh in range(C_HEADS)]
    nseq = len(seqs)

    qm = []
    for bb, hh in seqs:
        qp = q_ref[bb, :, (hh // 2) * BLOCK:(hh // 2 + 1) * BLOCK].astype(_F32)
        keep = first_head if hh % 2 == 0 else jnp.logical_not(first_head)
        qm.append(jnp.where(keep, qp, 0.0).astype(_BF))

    def blocks_of(t):
        base = (ngrp - 1 - t) * grp
        return [base + (grp - 1 - g) for g in range(grp)]

    def rows_of(j):
        return pl.ds(pl.multiple_of(jnp.clip(j, 0, nb - 1) * BLOCK, BLOCK), BLOCK)

    def pair_tiles(ref, j):
        return [[ref[bb, rows_of(j), pr * BLOCK:(pr + 1) * BLOCK] for pr in range(C_HEADS // 2)]
                for bb in range(nbt)]

    def qk(t):
        out = []
        for j in blocks_of(t):
            kps = pair_tiles(k_ref, j)
            out += [lax.dot_general(qm[n], kps[bb][hh // 2], (((1,), (1,)), ((), ())),
                                    preferred_element_type=_F32) for n, (bb, hh) in enumerate(seqs)]
        return out

    def wv(t, slot, acc):
        vs = [pair_tiles(v_ref, j) for j in blocks_of(t)]
        new = []
        for n, (bb, hh) in enumerate(seqs):
            wcat = jnp.concatenate([w_ref[slot, g * nseq + n] for g in range(grp)], axis=1)
            vcat = jnp.concatenate([vs[g][bb][hh // 2] for g in range(grp)], axis=0)
            new.append(acc[n] + jnp.dot(wcat, vcat, preferred_element_type=_F32))
        return new

    for n, z in enumerate(qk(0)):
        z_ref[0, n] = z
    w_ref[1] = jnp.zeros(w_ref.shape[1:], _BF)
    carry_ref[...] = jnp.zeros_like(carry_ref)
    acc_ref[...] = jnp.zeros_like(acc_ref)

    def body(t, _):
        slot = lax.rem(t, 2)
        acc = wv(t - 1, 1 - slot, [acc_ref[n] for n in range(nseq)])
        z_next = qk(t + 1)
        carry = [carry_ref[n] for n in range(nseq)]
        zs = [z_ref[slot, n] for n in range(grp * nseq)]

        lks, zzs, masks = [], [], []
        for g, j in enumerate(blocks_of(t)):
            kpos = j * BLOCK + lane
            strict = (kpos < row_t) & (kpos >= PAD)
            masks.append(strict)
            for n in range(nseq):
                z = zs[g * nseq + n]
                zz = jnp.minimum(z, 0.0) - jnp.log(1.0 + jnp.exp(-jnp.abs(z)))
                lks.append(jnp.where(strict, zz - z, 0.0).astype(_BF))
                zzs.append(zz)
        cums = [jnp.dot(lk, low_ref[...], preferred_element_type=_F32) for lk in lks]
        ws = []
        for g in range(grp):
            for n in range(nseq):
                c = g * nseq + n
                w = jnp.where(masks[g], jnp.exp(zzs[c] + cums[c][:, :BLOCK] + carry[n]), 0.0)
                ws.append(w.astype(_BF))
                carry[n] = carry[n] + cums[c][:, BLOCK:]

        for n in range(nseq):
            acc_ref[n] = acc[n]
            carry_ref[n] = carry[n]
        for c in range(grp * nseq):
            z_ref[1 - slot, c] = z_next[c]
            w_ref[slot, c] = ws[c]
        return 0

    lax.fori_loop(0, ngrp, body, 0)
    acc = wv(ngrp - 1, lax.rem(ngrp - 1, 2), [acc_ref[n] for n in range(nseq)])
    for bb in range(nbt):
        yc_ref[bb] = jnp.concatenate([jnp.where(first_head, acc[bb * C_HEADS + 2 * pr], acc[bb * C_HEADS + 2 * pr + 1])
                                      for pr in range(C_HEADS // 2)], axis=1).astype(_BF)


def _stick(qc3, kc3, vc3, low_ext):
    bsz, plen, _ = kc3.shape
    nb = plen // BLOCK
    nbt = STEP_BATCH
    nseq = nbt * C_HEADS
    nchain = STICK_GROUP * nseq
    seq_blk = lambda rows: pl.BlockSpec((nbt, rows, C_WIDTH), (lambda b, i: (b, i, 0)) if rows == BLOCK
                                        else (lambda b, i: (b, 0, 0)))
    return pl.pallas_call(
        functools.partial(_stick_kernel, nb),
        grid=(bsz // nbt, nb),
        in_specs=[seq_blk(BLOCK), seq_blk(plen), seq_blk(plen), _const_spec((BLOCK, 2 * BLOCK))],
        out_specs=seq_blk(BLOCK),
        out_shape=jax.ShapeDtypeStruct((bsz, plen, C_WIDTH), _BF),
        scratch_shapes=[pltpu.VMEM((2, nchain, BLOCK, BLOCK), _F32),
                        pltpu.VMEM((2, nchain, BLOCK, BLOCK), _BF),
                        pltpu.VMEM((nseq, BLOCK, BLOCK), _F32),
                        pltpu.VMEM((nseq, BLOCK, BLOCK), _F32)],
        compiler_params=pltpu.CompilerParams(dimension_semantics=("arbitrary", "arbitrary"),
                                             vmem_limit_bytes=VMEM_LIMIT),
        name="stick",
    )(qc3, kc3, vc3, low_ext).reshape(bsz * plen, C_WIDTH)


def _s5_kernel(bsz, u_ref, wbu_ref, a_ref, wc_ref, d_ref, wglu_ref, y_ref, bur_ref, bui_ref, st_ref):
    c = pl.program_id(0)
    slab_w = S5_LANES // S5_SLABS
    ch_w = B_WIDTH // S5_SLABS

    @pl.when(c == 0)
    def _():
        st_ref[...] = jnp.zeros_like(st_ref)

    u = u_ref[...]
    for k in range(S5_SLABS):
        o = jnp.dot(u[:, k * ch_w:(k + 1) * ch_w], wbu_ref[k], preferred_element_type=_F32)
        bur_ref[:, k * slab_w:(k + 1) * slab_w] = o[:, :slab_w]
        bui_ref[:, k * slab_w:(k + 1) * slab_w] = o[:, slab_w:]

    for k in range(S5_SLABS):
        ls = slice(k * slab_w, (k + 1) * slab_w)
        ar = jnp.broadcast_to(a_ref[0:1, ls], (bsz, slab_w))
        ai = jnp.broadcast_to(a_ref[1:2, ls], (bsz, slab_w))

        def step(t, st, ls=ls, ar=ar, ai=ai):
            hr, hi = st
            rows = pl.ds(pl.multiple_of(t * bsz, bsz), bsz)
            nhr = ar * hr - ai * hi + bur_ref[rows, ls]
            nhi = ar * hi + ai * hr + bui_ref[rows, ls]
            bur_ref[rows, ls] = nhr
            bui_ref[rows, ls] = nhi
            return nhr, nhi

        hr, hi = lax.fori_loop(0, S5_CHUNK, step, (st_ref[0, :, ls], st_ref[1, :, ls]), unroll=4)
        st_ref[0, :, ls] = hr
        st_ref[1, :, ls] = hi

    ys = []
    for k in range(S5_SLABS):
        ls = slice(k * slab_w, (k + 1) * slab_w)
        hcat = jnp.concatenate([bur_ref[:, ls].astype(_BF), bui_ref[:, ls].astype(_BF)], axis=1)
        ys.append(jnp.dot(hcat, wc_ref[k], preferred_element_type=_F32))
    y = jnp.concatenate(ys, axis=1) + d_ref[...] * u.astype(_F32)
    y = 0.5 * y * (1.0 + jnp.tanh(math.sqrt(2.0 / math.pi) * (y + 0.044715 * (y * y * y))))
    gate = jnp.dot(y.astype(_BF), wglu_ref[...], preferred_element_type=_F32)
    y_ref[...] = (y * (1.0 / (1.0 + jnp.exp(-gate)))).astype(_BF)


def _s5(bsz, u_tb, wbu, a_pack, wc, d_skip, w_glu):
    rows = u_tb.shape[0]
    rc = S5_CHUNK * bsz
    return pl.pallas_call(
        functools.partial(_s5_kernel, bsz),
        grid=(rows // rc,),
        in_specs=[
            pl.BlockSpec((rc, B_WIDTH), lambda c: (c, 0)),
            _const_spec((S5_SLABS, B_WIDTH // S5_SLABS, 2 * S5_LANES // S5_SLABS)),
            _const_spec((2, S5_LANES)),
            _const_spec((S5_SLABS, 2 * S5_LANES // S5_SLABS, B_WIDTH // S5_SLABS)),
            _const_spec((1, B_WIDTH)),
            _const_spec((B_WIDTH, B_WIDTH)),
        ],
        out_specs=pl.BlockSpec((rc, B_WIDTH), lambda c: (c, 0)),
        out_shape=jax.ShapeDtypeStruct((rows, B_WIDTH), _BF),
        scratch_shapes=[pltpu.VMEM((rc, S5_LANES), _F32), pltpu.VMEM((rc, S5_LANES), _F32),
                        pltpu.VMEM((2, bsz, S5_LANES), _F32)],
        compiler_params=pltpu.CompilerParams(dimension_semantics=("arbitrary",),
                                             vmem_limit_bytes=VMEM_LIMIT),
        name="s5",
    )(u_tb, wbu, a_pack, wc, d_skip, w_glu)


def _mix_mlp_kernel(plen, final, h_ref, ya_ref, yb_ref, yc_ref, ga_ref, gb_ref, gc_ref, wo_ref,
                    g2_ref, w1_ref, w2_ref, gf_ref, o_ref):
    tm = h_ref.shape[0]
    y = jnp.concatenate([_rms(ya_ref[...].astype(_F32), ga_ref[...]),
                         _rms(yb_ref[...].astype(_F32), gb_ref[...]),
                         _rms(yc_ref[...].astype(_F32), gc_ref[...])], axis=1).astype(_BF)
    h = h_ref[...] + jnp.dot(y, wo_ref[...], preferred_element_type=_F32)
    hn = _rms(h, g2_ref[...]).astype(_BF)
    ffc = D_FF // 4
    for cc in range(4):
        a = jnp.dot(hn, w1_ref[:, cc * ffc:(cc + 1) * ffc], preferred_element_type=_F32)
        a = jnp.square(jnp.maximum(a, 0.0)).astype(_BF)
        h = h + jnp.dot(a, w2_ref[cc * ffc:(cc + 1) * ffc, :], preferred_element_type=_F32)
    if final:
        o_ref[...] = _rms(h, gf_ref[...])
    else:
        r = (pl.program_id(0) * tm + lax.broadcasted_iota(jnp.int32, (tm, 1), 0)).astype(_F32)
        pos = r - jnp.floor((r + 0.5) * (1.0 / plen)) * plen
        o_ref[...] = jnp.where(pos >= PAD, h, 0.0)


def _mix_mlp(plen, final, h2d, ya, yb, yc, ga, gb, gc, wo, g2, w1, w2, gf):
    rows = h2d.shape[0]
    tm = ROW_TILE
    if final:
        seq = plen - BLOCK
        tiles = seq // tm
        grid = (rows // plen, tiles)
        row = lambda w: pl.BlockSpec((pl.Element(tm), pl.Element(w)),
                                     lambda b, j: (pl.multiple_of(b * plen + BLOCK + tm * j, BLOCK), 0))
        out_spec = pl.BlockSpec((tm, D_MODEL), lambda b, j: (b * tiles + j, 0))
        out_rows = (rows // plen) * seq
    else:
        grid = (rows // tm,)
        row = lambda w: pl.BlockSpec((tm, w), lambda i: (i, 0))
        out_spec, out_rows = row(D_MODEL), rows
    return pl.pallas_call(
        functools.partial(_mix_mlp_kernel, plen, final),
        grid=grid,
        in_specs=[row(D_MODEL), row(A_WIDTH), row(B_WIDTH), row(C_WIDTH),
                  _const_spec((1, A_WIDTH)), _const_spec((1, B_WIDTH)), _const_spec((1, C_WIDTH)),
                  _const_spec((D_MODEL, D_MODEL)), _const_spec((1, D_MODEL)),
                  _const_spec((D_MODEL, D_FF)), _const_spec((D_FF, D_MODEL)), _const_spec((1, D_MODEL))],
        out_specs=out_spec,
        out_shape=jax.ShapeDtypeStruct((out_rows, D_MODEL), _F32),
        compiler_params=pltpu.CompilerParams(dimension_semantics=("arbitrary",) * len(grid),
                                             vmem_limit_bytes=VMEM_LIMIT),
        name="mix_mlp",
    )(h2d, ya, yb, yc, ga, gb, gc, wo, g2, w1, w2, gf)


def _t5_bucket_tiles():
    max_exact = NUM_BUCKETS // 2
    r = jnp.arange(BLOCK)[:, None]
    c = jnp.arange(BLOCK)[None, :]
    tiles = []
    for dd in range(2):
        n = jnp.maximum(dd * BLOCK + r - c, 0)
        nf = jnp.maximum(n, max_exact).astype(_F32)
        large = max_exact + (jnp.log(nf / max_exact) / math.log(MAX_DISTANCE / max_exact)
                             * (NUM_BUCKETS - max_exact)).astype(jnp.int32)
        large = jnp.minimum(large, NUM_BUCKETS - 1)
        tiles.append(jnp.where(n < max_exact, n, large))
    return jnp.stack(tiles).astype(jnp.int32)


def _pack_w_in(w_in):
    sizes = (A_WIDTH, KV_RANK, IDX_HEADS * IDX_DIM, IDX_DIM, IDX_HEADS, B_WIDTH, C_WIDTH, C_WIDTH, C_WIDTH)
    offs = np.cumsum((0,) + sizes)
    qa, ckv, iq, ik, iw, ub, qc, kc, vc = [w_in[:, offs[n]:offs[n + 1]] for n in range(len(sizes))]
    zeros = lambda n: jnp.zeros((w_in.shape[0], n), w_in.dtype)
    cols = [qa, ckv]
    for hh in range(IDX_HEADS):
        cols += [iq[:, hh * IDX_DIM:(hh + 1) * IDX_DIM], zeros(128 - IDX_DIM)]
    cols += [ik, iw, zeros(128 - IDX_DIM - IDX_HEADS), ub, qc, kc, vc]
    return jnp.concatenate(cols, axis=1).astype(_BF)


def _block_diag(blocks):
    n, r, c = blocks.shape
    eye = jnp.eye(n, dtype=blocks.dtype)
    return (eye[:, None, :, None] * blocks[:, :, None, :]).reshape(n * r, n * c)


def _s5_params(lam_re, lam_im, log_dt, b_re, b_im, c_re, c_im):
    dt = jnp.exp(log_dt)[:, None]
    mag = jnp.exp(lam_re * dt)
    ar = mag * jnp.cos(lam_im * dt)
    ai = mag * jnp.sin(lam_im * dt)
    den = lam_re * lam_re + lam_im * lam_im
    nr, ni = ar - 1.0, ai
    fr = (nr * lam_re + ni * lam_im) / den
    fi = (ni * lam_re - nr * lam_im) / den
    bfr = fr[:, :, None] * b_re - fi[:, :, None] * b_im
    bfi = fr[:, :, None] * b_im + fi[:, :, None] * b_re
    gps = S5_GROUPS // S5_SLABS
    wbu, wc = [], []
    for k in range(S5_SLABS):
        gs = slice(k * gps, (k + 1) * gps)
        wbu.append(jnp.concatenate([_block_diag(jnp.swapaxes(bfr[gs], 1, 2)),
                                    _block_diag(jnp.swapaxes(bfi[gs], 1, 2))], axis=1))
        wc.append(jnp.concatenate([_block_diag(jnp.swapaxes(c_re[gs], 1, 2)),
                                   _block_diag(jnp.swapaxes(-c_im[gs], 1, 2))], axis=0))
    a_pack = jnp.stack([ar.reshape(-1), ai.reshape(-1)])
    return jnp.stack(wbu).astype(_BF), a_pack, jnp.stack(wc).astype(_BF)


def kernel(x, meta_tokens, rel_bias, norm1_g, w_in, kv_norm_g, w_uk, w_uv, lambda_re, lambda_im, log_dt,
           b_re, b_im, c_re, c_im, d_skip, w_glu, gn_a, gn_b, gn_c, w_out, norm2_g, w_mlp1, w_mlp2, final_g):
    bsz, seq, _ = x.shape
    depth = w_in.shape[0]
    n_keys = seq + N_META
    k_top = min(TOPK_MAX, n_keys // 4)
    plen = n_keys + PAD
    rows = bsz * plen
    assert seq % ROW_TILE == 0 and rows % ROW_TILE == 0 and plen % S5_CHUNK == 0 and bsz % 8 == 0
    assert bsz % STEP_BATCH == 0

    head = jnp.concatenate([jnp.zeros((PAD, D_MODEL), x.dtype), meta_tokens.astype(x.dtype)], axis=0)
    h = jnp.concatenate([jnp.broadcast_to(head[None], (bsz, BLOCK, D_MODEL)), x], axis=1).reshape(rows, D_MODEL)

    bucket_t = jnp.swapaxes(_t5_bucket_tiles(), 1, 2)
    ar_ = jnp.arange(BLOCK)
    lower = (ar_[:, None] > ar_[None, :]).astype(_BF)
    ones = jnp.ones((BLOCK, BLOCK), _BF)
    low_ext = jnp.concatenate([lower, ones], axis=1)
    row1 = lambda v: v.reshape(1, -1).astype(_F32)

    for l in range(depth):
        w_all = _pack_w_in(w_in[l])
        wuk_bd = _block_diag(jnp.transpose(w_uk[l], (1, 2, 0))).astype(_BF)
        wuvt_pad = jnp.stack([jnp.pad(w_uv[l][:, hh, :].T, ((hh * HEAD_DIM, A_WIDTH - (hh + 1) * HEAD_DIM), (0, 0)))
                              for hh in range(A_HEADS)]).astype(_BF)
        qlat4, ckv, iq4, ikw, ub, qc, kc, vc = _inproj(h, row1(norm1_g[l]), w_all, row1(kv_norm_g[l]), wuk_bd)

        ckv3 = ckv.reshape(bsz, plen, KV_RANK)
        gk = DSA_GROUP * BLOCK
        ckvt3 = jnp.swapaxes(jnp.pad(ckv3, ((0, 0), (0, -plen % gk), (0, 0))).reshape(bsz, -1, gk, KV_RANK), 2, 3)
        ya = _dsa(k_top, rel_bias.astype(_F32), qlat4, iq4, ikw.reshape(bsz, plen, 128),
                  ckv3, ckvt3, bucket_t, wuvt_pad, lower)

        wbu, a_pack, wc = _s5_params(lambda_re[l], lambda_im[l], log_dt[l], b_re[l], b_im[l], c_re[l], c_im[l])
        u_tb = jnp.swapaxes(ub.reshape(bsz, plen, B_WIDTH), 0, 1).reshape(rows, B_WIDTH)
        yb_tb = _s5(bsz, u_tb, wbu, a_pack, wc, row1(d_skip[l]), w_glu[l].astype(_BF))
        yb = jnp.swapaxes(yb_tb.reshape(plen, bsz, B_WIDTH), 0, 1).reshape(rows, B_WIDTH)

        yc = _stick(qc.reshape(bsz, plen, C_WIDTH), kc.reshape(bsz, plen, C_WIDTH),
                    vc.reshape(bsz, plen, C_WIDTH), low_ext)

        h = _mix_mlp(plen, l == depth - 1, h, ya, yb, yc, row1(gn_a[l]), row1(gn_b[l]), row1(gn_c[l]),
                     w_out[l].astype(_BF), row1(norm2_g[l]), w_mlp1[l].astype(_BF), w_mlp2[l].astype(_BF),
                     row1(final_g))

    return h.reshape(bsz, seq, D_MODEL)
```

```python
import functools
import math

import jax
import jax.numpy as jnp
import numpy as np
from jax import lax
from jax.experimental import pallas as pl
from jax.experimental.pallas import tpu as pltpu

D_MODEL = 1024
N_META = 16
BLOCK = 128
PAD = BLOCK - N_META
HEAD_DIM = 64
A_WIDTH = 256
A_HEADS = 4
KV_RANK = 128
IDX_HEADS = 4
IDX_DIM = 64
TOPK_MAX = 256
B_WIDTH = 512
S5_GROUP = 16
S5_GROUPS = 32
S5_STATE = 64
C_WIDTH = 256
C_HEADS = 4
D_FF = 4 * D_MODEL
NUM_BUCKETS = 32
MAX_DISTANCE = 128
RMS_EPS = 1e-6
NEG_INF = -1e30

ROW_TILE = 512
S5_CHUNK = 32
S5_LANES = S5_GROUPS * S5_STATE
S5_SLABS = 4
STEP_BATCH = 4
DSA_GROUP = 1
PV_GROUP = 4
BISECT_ITERS = 18
STICK_GROUP = 1
VMEM_LIMIT = 56 * 1024 * 1024

_BF = jnp.bfloat16
_F32 = jnp.float32

_C_QA = 0
_C_CKV = _C_QA + A_WIDTH
_C_IQ = _C_CKV + KV_RANK
_C_IKW = _C_IQ + IDX_HEADS * 128
_C_UB = _C_IKW + 128
_C_QC = _C_UB + B_WIDTH
_C_KC = _C_QC + C_WIDTH
_C_VC = _C_KC + C_WIDTH
_N_PROJ = _C_VC + C_WIDTH


def _rms(x, g):
    return x * lax.rsqrt(jnp.mean(x * x, axis=-1, keepdims=True) + RMS_EPS) * g


def _inproj_kernel(h_ref, g1_ref, w_ref, kvg_ref, wuk_ref,
                   qlat_ref, ckv_ref, iq_ref, ikw_ref, ub_ref, qc_ref, kc_ref, vc_ref):
    hn = _rms(h_ref[...], g1_ref[...]).astype(_BF)
    proj = jnp.dot(hn, w_ref[...], preferred_element_type=_F32)
    qa = proj[:, _C_QA:_C_QA + A_WIDTH].astype(_BF)
    qlat = jnp.dot(qa, wuk_ref[...], preferred_element_type=_F32) * (HEAD_DIM ** -0.5)
    for hh in range(A_HEADS):
        qlat_ref[hh] = qlat[:, hh * KV_RANK:(hh + 1) * KV_RANK].astype(_BF)
    ckv_ref[...] = _rms(proj[:, _C_CKV:_C_CKV + KV_RANK], kvg_ref[...]).astype(_BF)
    for hh in range(IDX_HEADS):
        c0 = _C_IQ + hh * 128
        iq_ref[hh] = proj[:, c0:c0 + IDX_DIM].astype(_BF)
    ikw_ref[...] = proj[:, _C_IKW:_C_IKW + 128]
    ub_ref[...] = proj[:, _C_UB:_C_UB + B_WIDTH].astype(_BF)
    qc_ref[...] = (proj[:, _C_QC:_C_QC + C_WIDTH] * (HEAD_DIM ** -0.5)).astype(_BF)
    kc_ref[...] = proj[:, _C_KC:_C_KC + C_WIDTH].astype(_BF)
    vc_ref[...] = proj[:, _C_VC:_C_VC + C_WIDTH].astype(_BF)


def _const_spec(shape):
    nd = len(shape)
    return pl.BlockSpec(shape, lambda *_: (0,) * nd, pipeline_mode=pl.Buffered(1))


def _inproj(h2d, g1, w_all, kvg, wuk_bd):
    rows = h2d.shape[0]
    tm = ROW_TILE
    row = lambda w: pl.BlockSpec((tm, w), lambda i: (i, 0))
    head = lambda w: pl.BlockSpec((4, tm, w), lambda i: (0, i, 0))
    out_shape = (
        jax.ShapeDtypeStruct((A_HEADS, rows, KV_RANK), _BF),
        jax.ShapeDtypeStruct((rows, KV_RANK), _BF),
        jax.ShapeDtypeStruct((IDX_HEADS, rows, IDX_DIM), _BF),
        jax.ShapeDtypeStruct((rows, 128), _F32),
        jax.ShapeDtypeStruct((rows, B_WIDTH), _BF),
        jax.ShapeDtypeStruct((rows, C_WIDTH), _BF),
        jax.ShapeDtypeStruct((rows, C_WIDTH), _BF),
        jax.ShapeDtypeStruct((rows, C_WIDTH), _BF),
    )
    return pl.pallas_call(
        _inproj_kernel,
        grid=(rows // tm,),
        in_specs=[row(D_MODEL), _const_spec((1, D_MODEL)), _const_spec((D_MODEL, _N_PROJ)),
                  _const_spec((1, KV_RANK)), _const_spec((A_WIDTH, A_HEADS * KV_RANK))],
        out_specs=(head(KV_RANK), row(KV_RANK), head(IDX_DIM), row(128), row(B_WIDTH),
                   row(C_WIDTH), row(C_WIDTH), row(C_WIDTH)),
        out_shape=out_shape,
        compiler_params=pltpu.CompilerParams(dimension_semantics=("arbitrary",),
                                             vmem_limit_bytes=VMEM_LIMIT),
        name="inproj",
    )(h2d, g1, w_all, kvg, wuk_bd)


def _dsa_kernel(k_top, nb, relb_ref, qlat_ref, iq_ref, ikw_ref, ckv_ref, ckvt_ref, bucket_ref, wuvt_ref,
                early_ref, ya_ref, score_ref, logit_ref, p_ref, acc_ref, bias_ref):
    grp = DSA_GROUP
    nbt = STEP_BATCH
    i = pl.program_id(1)
    ngrp = (i + grp) // grp
    kf = float(k_top)
    hw = A_HEADS * BLOCK
    bbs = range(nbt)

    @pl.when(i == 0)
    def _():
        for dd in range(2):
            bk = bucket_ref[dd]
            for hh in range(A_HEADS):
                acc = jnp.zeros((BLOCK, BLOCK), _F32)
                for b_ in range(NUM_BUCKETS):
                    acc = jnp.where(bk == b_, relb_ref[b_, hh], acc)
                bias_ref[dd, :, hh * BLOCK:(hh + 1) * BLOCK] = acc
        for hh in range(A_HEADS):
            bias_ref[2, :, hh * BLOCK:(hh + 1) * BLOCK] = jnp.full((BLOCK, BLOCK), relb_ref[NUM_BUCKETS - 1, hh], _F32)

    key_s = lax.broadcasted_iota(jnp.int32, (BLOCK, BLOCK), 0)
    q_t = i * BLOCK + lax.broadcasted_iota(jnp.int32, (BLOCK, BLOCK), 1)

    def blocks_of(t):
        return [grp * t + g for g in range(grp)]

    def rows_of(j):
        return pl.ds(pl.multiple_of(jnp.minimum(j, nb - 1) * BLOCK, BLOCK), BLOCK)

    def col_sum(x):
        return jnp.sum(x, axis=0, keepdims=True)

    def fold8(x, op):
        return op(x.reshape(BLOCK // 8, 8, BLOCK), axis=0)

    def row_any(flags):
        return functools.reduce(jnp.maximum, [jnp.max(f) for f in flags]) > 0.0

    iq = [iq_ref[:, bb].reshape(hw, IDX_DIM) for bb in bbs]
    q4 = [qlat_ref[:, bb].reshape(hw, KV_RANK) for bb in bbs]
    wrow = []
    for bb in bbs:
        wq_t = ikw_ref[bb, pl.ds(pl.multiple_of(i * BLOCK, BLOCK), BLOCK), :].T
        wrow.append([wq_t[IDX_DIM + hh:IDX_DIM + hh + 1, :] * (IDX_DIM ** -0.5 * IDX_HEADS ** -0.5)
                     for hh in range(IDX_HEADS)])
    nt = (((1,), (1,)), ((), ()))

    def score_body(t, carry):
        mn, mx = list(carry[0]), list(carry[1])
        js = blocks_of(t)
        work = [(bb, j) for bb in bbs for j in js]
        iks = [ikw_ref[bb, rows_of(j), :][:, :IDX_DIM].astype(_BF) for bb, j in work]
        ckvs = [ckv_ref[bb, rows_of(j), :] for bb, j in work]
        raws = [lax.dot_general(ik, iq[bb], nt, preferred_element_type=_F32) for ik, (bb, j) in zip(iks, work)]
        lts = [lax.dot_general(ckv, q4[bb], nt, preferred_element_type=_F32) for ckv, (bb, j) in zip(ckvs, work)]
        outs = []
        for (bb, j), raw in zip(work, raws):
            s = jnp.zeros((BLOCK, BLOCK), _F32)
            for hh in range(IDX_HEADS):
                s = s + jnp.maximum(raw[:, hh * BLOCK:(hh + 1) * BLOCK], 0.0) * wrow[bb][hh]
            s = jnp.where(s == 0.0, 0.0, s)
            kpos = j * BLOCK + key_s
            adm = (kpos <= q_t) & (kpos >= PAD)
            s_lo = jnp.where(adm, s, -jnp.inf)
            mn[bb] = jnp.minimum(mn[bb], fold8(jnp.where(adm, s, jnp.inf), jnp.min))
            mx[bb] = jnp.maximum(mx[bb], fold8(s_lo, jnp.max))
            outs.append(s_lo)
        for (bb, j), s_lo, lt in zip(work, outs, lts):
            score_ref[bb, j] = s_lo
            logit_ref[bb, j] = lt
        return mn, mx

    mn, mx = lax.fori_loop(0, ngrp, score_body, ([jnp.full((8, BLOCK), jnp.inf, _F32)] * nbt,
                                                 [jnp.full((8, BLOCK), -jnp.inf, _F32)] * nbt))
    s_min = [jnp.min(x, axis=0, keepdims=True) for x in mn]
    s_max = [jnp.max(x, axis=0, keepdims=True) for x in mx]

    n_adm = (i * BLOCK + lax.broadcasted_iota(jnp.int32, (1, BLOCK), 1) - (PAD - 1)).astype(_F32)
    take_all = n_adm <= kf

    def count_where(pred):
        def body(t, acc):
            acc = list(acc)
            for bb in bbs:
                for j in blocks_of(t):
                    acc[bb] = acc[bb] + fold8(jnp.where(pred(bb, score_ref[bb, j]), 1.0, 0.0), jnp.sum)
            return acc
        acc = lax.fori_loop(0, ngrp, body, [jnp.zeros((8, BLOCK), _F32)] * nbt)
        return [col_sum(a) for a in acc]

    def max_where(pred):
        def body(t, m):
            m = list(m)
            for bb in bbs:
                for j in blocks_of(t):
                    s = score_ref[bb, j]
                    m[bb] = jnp.maximum(m[bb], fold8(jnp.where(pred(bb, s), s, -jnp.inf), jnp.max))
            return m
        m = lax.fori_loop(0, ngrp, body, [jnp.full((8, BLOCK), -jnp.inf, _F32)] * nbt)
        return [jnp.max(x, axis=0, keepdims=True) for x in m]

    def select_threshold():
        def bis(_, st):
            lo, hi, c_hi = st
            mid = [(lo[bb] + hi[bb]) * 0.5 for bb in bbs]
            c = count_where(lambda bb, s: s > mid[bb])
            take = [c[bb] < kf for bb in bbs]
            return ([jnp.where(take[bb], lo[bb], mid[bb]) for bb in bbs],
                    [jnp.where(take[bb], mid[bb], hi[bb]) for bb in bbs],
                    [jnp.where(take[bb], c[bb], c_hi[bb]) for bb in bbs])

        _, hi_f, c_gt = lax.fori_loop(0, BISECT_ITERS, bis, (s_min, s_max, [jnp.zeros((1, BLOCK), _F32)] * nbt))
        thr = max_where(lambda bb, s: s <= hi_f[bb])
        g = count_where(lambda bb, s: s == thr[bb])

        def not_done(st):
            thr, c_gt, g, it = st
            pending = [jnp.where(take_all, 0.0, jnp.where(c_gt[bb] + g[bb] >= kf, 0.0, 1.0)) for bb in bbs]
            return jnp.logical_and(row_any(pending), it < 4096)

        def peel(st):
            thr, c_gt, g, it = st
            done = [jnp.logical_or(take_all, c_gt[bb] + g[bb] >= kf) for bb in bbs]
            thr_n = max_where(lambda bb, s: s < thr[bb])
            g_n = count_where(lambda bb, s: s == thr_n[bb])
            return ([jnp.where(done[bb], thr[bb], thr_n[bb]) for bb in bbs],
                    [jnp.where(done[bb], c_gt[bb], c_gt[bb] + g[bb]) for bb in bbs],
                    [jnp.where(done[bb], g[bb], g_n[bb]) for bb in bbs], it + 1)

        thr, c_gt, g, _ = lax.while_loop(not_done, peel, (thr, c_gt, g, jnp.int32(0)))
        return thr, c_gt, g

    zrow = jnp.zeros((1, BLOCK), _F32)
    thr, c_gt, g = lax.cond(i * BLOCK + (BLOCK - 1) - (PAD - 1) > k_top, select_threshold,
                            lambda: ([zrow] * nbt, [zrow] * nbt, [zrow] * nbt))
    thr = [jnp.where(take_all, -jnp.inf, x) for x in thr]
    need = [kf - x for x in c_gt]
    surplus_ties = row_any([jnp.where(take_all, 0.0, jnp.where(c_gt[bb] + g[bb] > kf, 1.0, 0.0)) for bb in bbs])

    def mask_pass(rank_ties):
        def body(t, st):
            eq_before, m8 = list(st[0]), [list(x) for x in st[1]]
            js = blocks_of(t)
            work = [(bb, j) for bb in bbs for j in js]
            ss = [score_ref[bb, j] for bb, j in work]
            lts = [logit_ref[bb, j] for bb, j in work]
            biases = [bias_ref[jnp.clip(i - j, 0, 2)] for j in js]
            blks = []
            for n, (bb, j) in enumerate(work):
                s, lt, bias = ss[n], lts[n], biases[n % grp]
                if rank_ties:
                    eq = (s == thr[bb]) & (s > -jnp.inf)
                    eqf = jnp.where(eq, 1.0, 0.0)
                    rank = eq_before[bb] + jnp.dot(early_ref[...], eqf.astype(_BF), preferred_element_type=_F32)
                    sel = (s > thr[bb]) | (eq & (rank < need[bb]))
                    eq_before[bb] = eq_before[bb] + col_sum(eqf)
                else:
                    sel = (s >= thr[bb]) & (s > -jnp.inf)
                row = []
                for hh in range(A_HEADS):
                    hs = slice(hh * BLOCK, (hh + 1) * BLOCK)
                    blk = jnp.where(sel, lt[:, hs] + bias[:, hs], NEG_INF)
                    m8[bb][hh] = jnp.maximum(m8[bb][hh], fold8(blk, jnp.max))
                    row.append(blk)
                blks.append(row)
            for (bb, j), row in zip(work, blks):
                for hh in range(A_HEADS):
                    logit_ref[bb, j, :, hh * BLOCK:(hh + 1) * BLOCK] = row[hh]
            return eq_before, m8

        _, m8 = lax.fori_loop(0, ngrp, body,
                              ([zrow] * nbt, [[jnp.full((8, BLOCK), NEG_INF, _F32)] * A_HEADS] * nbt))
        return m8

    m8 = lax.cond(surplus_ties, lambda: mask_pass(True), lambda: mask_pass(False))
    m = [jnp.concatenate([jnp.max(x, axis=0, keepdims=True) for x in m8[bb]], axis=1) for bb in bbs]

    pgrp = PV_GROUP
    npg = (i + pgrp) // pgrp

    def pad_block(j, _):
        for bb in bbs:
            logit_ref[bb, j] = jnp.full((BLOCK, hw), NEG_INF, _F32)
        return 0

    lax.fori_loop(ngrp * grp, npg * pgrp, pad_block, 0)
    p_ref[1] = jnp.zeros(p_ref.shape[1:], _BF)
    acc_ref[...] = jnp.zeros_like(acc_ref)

    def pv_body(t, l8):
        slot = lax.rem(t, 2)
        l8 = [list(x) for x in l8]
        accs = [acc_ref[bb] + jnp.dot(ckvt_ref[bb, jnp.maximum(t - 1, 0)], p_ref[1 - slot, bb],
                                      preferred_element_type=_F32) for bb in bbs]
        ps = [[jnp.exp(logit_ref[bb, pgrp * t + g] - m[bb]) for g in range(pgrp)] for bb in bbs]
        for bb in bbs:
            for hh in range(A_HEADS):
                l8[bb][hh] = l8[bb][hh] + sum(fold8(p[:, hh * BLOCK:(hh + 1) * BLOCK], jnp.sum) for p in ps[bb])
        for bb in bbs:
            acc_ref[bb] = accs[bb]
            p_ref[slot, bb] = jnp.concatenate([p.astype(_BF) for p in ps[bb]], axis=0)
        return l8

    l8 = lax.fori_loop(0, npg, pv_body, [[jnp.zeros((8, BLOCK), _F32)] * A_HEADS] * nbt)
    for bb in bbs:
        acc = acc_ref[bb] + jnp.dot(ckvt_ref[bb, npg - 1], p_ref[lax.rem(npg - 1, 2), bb],
                                    preferred_element_type=_F32)
        l = jnp.concatenate([col_sum(x) for x in l8[bb]], axis=1)
        o_lat = (acc / l).astype(_BF)
        out_t = jnp.zeros((A_WIDTH, BLOCK), _F32)
        for hh in range(A_HEADS):
            out_t = out_t + jnp.dot(wuvt_ref[hh], o_lat[:, hh * BLOCK:(hh + 1) * BLOCK], preferred_element_type=_F32)
        ya_ref[bb] = out_t.T.astype(_BF)


def _dsa(k_top, rel_bias, qlat4, iq4, ikw3, ckv3, ckvt4, bucket_t, wuvt_pad, early):
    bsz, plen, _ = ckv3.shape
    nb = plen // BLOCK
    nbt = STEP_BATCH
    nblk = -(-nb // DSA_GROUP) * DSA_GROUP
    ngmax = -(-nb // PV_GROUP)
    nblk_pv = max(nblk, ngmax * PV_GROUP)
    gk = PV_GROUP * BLOCK
    hw = A_HEADS * BLOCK
    assert ckvt4.shape == (bsz, ngmax, KV_RANK, gk) and bsz % nbt == 0
    return pl.pallas_call(
        functools.partial(_dsa_kernel, k_top, nb),
        grid=(bsz // nbt, nb),
        in_specs=[
            pl.BlockSpec(memory_space=pltpu.SMEM),
            pl.BlockSpec((A_HEADS, nbt, BLOCK, KV_RANK), lambda b, i: (0, b, i, 0)),
            pl.BlockSpec((IDX_HEADS, nbt, BLOCK, IDX_DIM), lambda b, i: (0, b, i, 0)),
            pl.BlockSpec((nbt, plen, 128), lambda b, i: (b, 0, 0)),
            pl.BlockSpec((nbt, plen, KV_RANK), lambda b, i: (b, 0, 0)),
            pl.BlockSpec((nbt, ngmax, KV_RANK, gk), lambda b, i: (b, 0, 0, 0)),
            _const_spec((2, BLOCK, BLOCK)),
            _const_spec((A_HEADS, A_WIDTH, KV_RANK)),
            _const_spec((BLOCK, BLOCK)),
        ],
        out_specs=pl.BlockSpec((nbt, BLOCK, A_WIDTH), lambda b, i: (b, i, 0)),
        out_shape=jax.ShapeDtypeStruct((bsz, plen, A_WIDTH), _BF),
        scratch_shapes=[pltpu.VMEM((nbt, nblk, BLOCK, BLOCK), _F32),
                        pltpu.VMEM((nbt, nblk_pv, BLOCK, hw), _F32),
                        pltpu.VMEM((2, nbt, gk, hw), _BF),
                        pltpu.VMEM((nbt, KV_RANK, hw), _F32),
                        pltpu.VMEM((3, BLOCK, hw), _F32)],
        compiler_params=pltpu.CompilerParams(dimension_semantics=("arbitrary", "arbitrary"),
                                             vmem_limit_bytes=VMEM_LIMIT),
        name="dsa",
    )(rel_bias, qlat4.reshape(A_HEADS, bsz, plen, KV_RANK), iq4.reshape(IDX_HEADS, bsz, plen, IDX_DIM),
      ikw3, ckv3, ckvt4, bucket_t, wuvt_pad, early).reshape(bsz * plen, A_WIDTH)


def _stick_kernel(nb, q_ref, k_ref, v_ref, low_ref, yc_ref, z_ref, w_ref, carry_ref, acc_ref):
    grp = STICK_GROUP
    nbt = STEP_BATCH
    i = pl.program_id(1)
    ngrp = (i + grp) // grp
    row_t = i * BLOCK + lax.broadcasted_iota(jnp.int32, (BLOCK, BLOCK), 0)
    lane = lax.broadcasted_iota(jnp.int32, (BLOCK, BLOCK), 1)
    first_head = lane < HEAD_DIM
    seqs = [(bb, hh) for bb in range(nbt) for hh in range(C_HEADS)]
    nseq = len(seqs)

    qm = []
    for bb, hh in seqs:
        qp = q_ref[bb, :, (hh // 2) * BLOCK:(hh // 2 + 1) * BLOCK].astype(_F32)
        keep = first_head if hh % 2 == 0 else jnp.logical_not(first_head)
        qm.append(jnp.where(keep, qp, 0.0).astype(_BF))

    def blocks_of(t):
        base = (ngrp - 1 - t) * grp
        return [base + (grp - 1 - g) for g in range(grp)]

    def rows_of(j):
        return pl.ds(pl.multiple_of(jnp.clip(j, 0, nb - 1) * BLOCK, BLOCK), BLOCK)

    def pair_tiles(ref, j):
        return [[ref[bb, rows_of(j), pr * BLOCK:(pr + 1) * BLOCK] for pr in range(C_HEADS // 2)]
                for bb in range(nbt)]

    def qk(t):
        out = []
        for j in blocks_of(t):
            kps = pair_tiles(k_ref, j)
            out += [lax.dot_general(qm[n], kps[bb][hh // 2], (((1,), (1,)), ((), ())),
                                    preferred_element_type=_F32) for n, (bb, hh) in enumerate(seqs)]
        return out

    def wv(t, slot, acc):
        vs = [pair_tiles(v_ref, j) for j in blocks_of(t)]
        new = []
        for n, (bb, hh) in enumerate(seqs):
            wcat = jnp.concatenate([w_ref[slot, g * nseq + n] for g in range(grp)], axis=1)
            vcat = jnp.concatenate([vs[g][bb][hh // 2] for g in range(grp)], axis=0)
            new.append(acc[n] + jnp.dot(wcat, vcat, preferred_element_type=_F32))
        return new

    for n, z in enumerate(qk(0)):
        z_ref[0, n] = z
    w_ref[1] = jnp.zeros(w_ref.shape[1:], _BF)
    carry_ref[...] = jnp.zeros_like(carry_ref)
    acc_ref[...] = jnp.zeros_like(acc_ref)

    def trip(t, masked):
        slot = lax.rem(t, 2)
        acc = wv(t - 1, 1 - slot, [acc_ref[n] for n in range(nseq)])
        z_next = qk(t + 1)
        carry = [carry_ref[n] for n in range(nseq)]
        zs = [z_ref[slot, n] for n in range(grp * nseq)]

        lks, zzs, masks = [], [], []
        for g, j in enumerate(blocks_of(t)):
            kpos = j * BLOCK + lane
            strict = (kpos < row_t) & (kpos >= PAD)
            masks.append(strict)
            for n in range(nseq):
                z = zs[g * nseq + n]
                zz = jnp.minimum(z, 0.0) - jnp.log(1.0 + jnp.exp(-jnp.abs(z)))
                lks.append((jnp.where(strict, zz - z, 0.0) if masked else zz - z).astype(_BF))
                zzs.append(zz)
        cums = [jnp.dot(lk, low_ref[...], preferred_element_type=_F32) for lk in lks]
        ws = []
        for g in range(grp):
            for n in range(nseq):
                c = g * nseq + n
                w = jnp.exp(zzs[c] + cums[c][:, :BLOCK] + carry[n])
                ws.append((jnp.where(masks[g], w, 0.0) if masked else w).astype(_BF))
                carry[n] = carry[n] + cums[c][:, BLOCK:]

        for n in range(nseq):
            acc_ref[n] = acc[n]
            carry_ref[n] = carry[n]
        for c in range(grp * nseq):
            z_ref[1 - slot, c] = z_next[c]
            w_ref[slot, c] = ws[c]

    def plain_trip(t, _):
        trip(t, False)
        return 0

    trip(0, True)
    lax.fori_loop(1, ngrp - 1, plain_trip, 0)

    @pl.when(ngrp >= 2)
    def _():
        trip(ngrp - 1, True)

    acc = wv(ngrp - 1, lax.rem(ngrp - 1, 2), [acc_ref[n] for n in range(nseq)])
    for bb in range(nbt):
        yc_ref[bb] = jnp.concatenate([jnp.where(first_head, acc[bb * C_HEADS + 2 * pr], acc[bb * C_HEADS + 2 * pr + 1])
                                      for pr in range(C_HEADS // 2)], axis=1).astype(_BF)


def _stick(qc3, kc3, vc3, low_ext):
    bsz, plen, _ = kc3.shape
    nb = plen // BLOCK
    nbt = STEP_BATCH
    nseq = nbt * C_HEADS
    nchain = STICK_GROUP * nseq
    seq_blk = lambda rows: pl.BlockSpec((nbt, rows, C_WIDTH), (lambda b, i: (b, i, 0)) if rows == BLOCK
                                        else (lambda b, i: (b, 0, 0)))
    return pl.pallas_call(
        functools.partial(_stick_kernel, nb),
        grid=(bsz // nbt, nb),
        in_specs=[seq_blk(BLOCK), seq_blk(plen), seq_blk(plen), _const_spec((BLOCK, 2 * BLOCK))],
        out_specs=seq_blk(BLOCK),
        out_shape=jax.ShapeDtypeStruct((bsz, plen, C_WIDTH), _BF),
        scratch_shapes=[pltpu.VMEM((2, nchain, BLOCK, BLOCK), _F32),
                        pltpu.VMEM((2, nchain, BLOCK, BLOCK), _BF),
                        pltpu.VMEM((nseq, BLOCK, BLOCK), _F32),
                        pltpu.VMEM((nseq, BLOCK, BLOCK), _F32)],
        compiler_params=pltpu.CompilerParams(dimension_semantics=("arbitrary", "arbitrary"),
                                             vmem_limit_bytes=VMEM_LIMIT),
        name="stick",
    )(qc3, kc3, vc3, low_ext).reshape(bsz * plen, C_WIDTH)


def _s5_kernel(bsz, u_ref, wbu_ref, a_ref, wc_ref, d_ref, wglu_ref, y_ref, bur_ref, bui_ref, st_ref):
    c = pl.program_id(0)
    slab_w = S5_LANES // S5_SLABS
    ch_w = B_WIDTH // S5_SLABS

    @pl.when(c == 0)
    def _():
        st_ref[...] = jnp.zeros_like(st_ref)

    u = u_ref[...]
    ys = []
    for k in range(S5_SLABS):
        ls = slice(k * slab_w, (k + 1) * slab_w)
        o = jnp.dot(u[:, k * ch_w:(k + 1) * ch_w], wbu_ref[k], preferred_element_type=_F32)
        bur_ref[:, ls] = o[:, :slab_w]
        bui_ref[:, ls] = o[:, slab_w:]
        ar = jnp.broadcast_to(a_ref[0:1, ls], (bsz, slab_w))
        ai = jnp.broadcast_to(a_ref[1:2, ls], (bsz, slab_w))
        hr, hi = st_ref[0, :, ls], st_ref[1, :, ls]
        for t in range(S5_CHUNK):
            rows = slice(t * bsz, (t + 1) * bsz)
            hr, hi = (ar * hr - ai * hi + bur_ref[rows, ls], ar * hi + ai * hr + bui_ref[rows, ls])
            bur_ref[rows, ls] = hr
            bui_ref[rows, ls] = hi
        st_ref[0, :, ls] = hr
        st_ref[1, :, ls] = hi
        hcat = jnp.concatenate([bur_ref[:, ls].astype(_BF), bui_ref[:, ls].astype(_BF)], axis=1)
        ys.append(jnp.dot(hcat, wc_ref[k], preferred_element_type=_F32))
    y = jnp.concatenate(ys, axis=1) + d_ref[...] * u.astype(_F32)
    y = 0.5 * y * (1.0 + jnp.tanh(math.sqrt(2.0 / math.pi) * (y + 0.044715 * (y * y * y))))
    gate = jnp.dot(y.astype(_BF), wglu_ref[...], preferred_element_type=_F32)
    y_ref[...] = (y * (1.0 / (1.0 + jnp.exp(-gate)))).astype(_BF)


def _s5(bsz, u_tb, wbu, a_pack, wc, d_skip, w_glu):
    rows = u_tb.shape[0]
    rc = S5_CHUNK * bsz
    return pl.pallas_call(
        functools.partial(_s5_kernel, bsz),
        grid=(rows // rc,),
        in_specs=[
            pl.BlockSpec((rc, B_WIDTH), lambda c: (c, 0)),
            _const_spec((S5_SLABS, B_WIDTH // S5_SLABS, 2 * S5_LANES // S5_SLABS)),
            _const_spec((2, S5_LANES)),
            _const_spec((S5_SLABS, 2 * S5_LANES // S5_SLABS, B_WIDTH // S5_SLABS)),
            _const_spec((1, B_WIDTH)),
            _const_spec((B_WIDTH, B_WIDTH)),
        ],
        out_specs=pl.BlockSpec((rc, B_WIDTH), lambda c: (c, 0)),
        out_shape=jax.ShapeDtypeStruct((rows, B_WIDTH), _BF),
        scratch_shapes=[pltpu.VMEM((rc, S5_LANES), _F32), pltpu.VMEM((rc, S5_LANES), _F32),
                        pltpu.VMEM((2, bsz, S5_LANES), _F32)],
        compiler_params=pltpu.CompilerParams(dimension_semantics=("arbitrary",),
                                             vmem_limit_bytes=VMEM_LIMIT),
        name="s5",
    )(u_tb, wbu, a_pack, wc, d_skip, w_glu)


def _mix_mlp_kernel(plen, final, h_ref, ya_ref, yb_ref, yc_ref, ga_ref, gb_ref, gc_ref, wo_ref,
                    g2_ref, w1_ref, w2_ref, gf_ref, o_ref):
    tm = h_ref.shape[0]
    y = jnp.concatenate([_rms(ya_ref[...].astype(_F32), ga_ref[...]),
                         _rms(yb_ref[...].astype(_F32), gb_ref[...]),
                         _rms(yc_ref[...].astype(_F32), gc_ref[...])], axis=1).astype(_BF)
    h = h_ref[...] + jnp.dot(y, wo_ref[...], preferred_element_type=_F32)
    hn = _rms(h, g2_ref[...]).astype(_BF)
    ffc = D_FF // 4
    for cc in range(4):
        a = jnp.dot(hn, w1_ref[:, cc * ffc:(cc + 1) * ffc], preferred_element_type=_F32)
        a = jnp.square(jnp.maximum(a, 0.0)).astype(_BF)
        h = h + jnp.dot(a, w2_ref[cc * ffc:(cc + 1) * ffc, :], preferred_element_type=_F32)
    if final:
        o_ref[...] = _rms(h, gf_ref[...])
    else:
        r = (pl.program_id(0) * tm + lax.broadcasted_iota(jnp.int32, (tm, 1), 0)).astype(_F32)
        pos = r - jnp.floor((r + 0.5) * (1.0 / plen)) * plen
        o_ref[...] = jnp.where(pos >= PAD, h, 0.0)


def _mix_mlp(plen, final, h2d, ya, yb, yc, ga, gb, gc, wo, g2, w1, w2, gf):
    rows = h2d.shape[0]
    tm = ROW_TILE
    if final:
        seq = plen - BLOCK
        tiles = seq // tm
        grid = (rows // plen, tiles)
        row = lambda w: pl.BlockSpec((pl.Element(tm), pl.Element(w)),
                                     lambda b, j: (pl.multiple_of(b * plen + BLOCK + tm * j, BLOCK), 0))
        out_spec = pl.BlockSpec((tm, D_MODEL), lambda b, j: (b * tiles + j, 0))
        out_rows = (rows // plen) * seq
    else:
        grid = (rows // tm,)
        row = lambda w: pl.BlockSpec((tm, w), lambda i: (i, 0))
        out_spec, out_rows = row(D_MODEL), rows
    return pl.pallas_call(
        functools.partial(_mix_mlp_kernel, plen, final),
        grid=grid,
        in_specs=[row(D_MODEL), row(A_WIDTH), row(B_WIDTH), row(C_WIDTH),
                  _const_spec((1, A_WIDTH)), _const_spec((1, B_WIDTH)), _const_spec((1, C_WIDTH)),
                  _const_spec((D_MODEL, D_MODEL)), _const_spec((1, D_MODEL)),
                  _const_spec((D_MODEL, D_FF)), _const_spec((D_FF, D_MODEL)), _const_spec((1, D_MODEL))],
        out_specs=out_spec,
        out_shape=jax.ShapeDtypeStruct((out_rows, D_MODEL), _F32),
        compiler_params=pltpu.CompilerParams(dimension_semantics=("arbitrary",) * len(grid),
                                             vmem_limit_bytes=VMEM_LIMIT),
        name="mix_mlp",
    )(h2d, ya, yb, yc, ga, gb, gc, wo, g2, w1, w2, gf)


def _t5_bucket_tiles():
    max_exact = NUM_BUCKETS // 2
    r = jnp.arange(BLOCK)[:, None]
    c = jnp.arange(BLOCK)[None, :]
    tiles = []
    for dd in range(2):
        n = jnp.maximum(dd * BLOCK + r - c, 0)
        nf = jnp.maximum(n, max_exact).astype(_F32)
        large = max_exact + (jnp.log(nf / max_exact) / math.log(MAX_DISTANCE / max_exact)
                             * (NUM_BUCKETS - max_exact)).astype(jnp.int32)
        large = jnp.minimum(large, NUM_BUCKETS - 1)
        tiles.append(jnp.where(n < max_exact, n, large))
    return jnp.stack(tiles).astype(jnp.int32)


def _pack_w_in(w_in):
    sizes = (A_WIDTH, KV_RANK, IDX_HEADS * IDX_DIM, IDX_DIM, IDX_HEADS, B_WIDTH, C_WIDTH, C_WIDTH, C_WIDTH)
    offs = np.cumsum((0,) + sizes)
    qa, ckv, iq, ik, iw, ub, qc, kc, vc = [w_in[:, offs[n]:offs[n + 1]] for n in range(len(sizes))]
    zeros = lambda n: jnp.zeros((w_in.shape[0], n), w_in.dtype)
    cols = [qa, ckv]
    for hh in range(IDX_HEADS):
        cols += [iq[:, hh * IDX_DIM:(hh + 1) * IDX_DIM], zeros(128 - IDX_DIM)]
    cols += [ik, iw, zeros(128 - IDX_DIM - IDX_HEADS), ub, qc, kc, vc]
    return jnp.concatenate(cols, axis=1).astype(_BF)


def _block_diag(blocks):
    n, r, c = blocks.shape
    eye = jnp.eye(n, dtype=blocks.dtype)
    return (eye[:, None, :, None] * blocks[:, :, None, :]).reshape(n * r, n * c)


def _s5_params(lam_re, lam_im, log_dt, b_re, b_im, c_re, c_im):
    dt = jnp.exp(log_dt)[:, None]
    mag = jnp.exp(lam_re * dt)
    ar = mag * jnp.cos(lam_im * dt)
    ai = mag * jnp.sin(lam_im * dt)
    den = lam_re * lam_re + lam_im * lam_im
    nr, ni = ar - 1.0, ai
    fr = (nr * lam_re + ni * lam_im) / den
    fi = (ni * lam_re - nr * lam_im) / den
    bfr = fr[:, :, None] * b_re - fi[:, :, None] * b_im
    bfi = fr[:, :, None] * b_im + fi[:, :, None] * b_re
    gps = S5_GROUPS // S5_SLABS
    wbu, wc = [], []
    for k in range(S5_SLABS):
        gs = slice(k * gps, (k + 1) * gps)
        wbu.append(jnp.concatenate([_block_diag(jnp.swapaxes(bfr[gs], 1, 2)),
                                    _block_diag(jnp.swapaxes(bfi[gs], 1, 2))], axis=1))
        wc.append(jnp.concatenate([_block_diag(jnp.swapaxes(c_re[gs], 1, 2)),
                                   _block_diag(jnp.swapaxes(-c_im[gs], 1, 2))], axis=0))
    a_pack = jnp.stack([ar.reshape(-1), ai.reshape(-1)])
    return jnp.stack(wbu).astype(_BF), a_pack, jnp.stack(wc).astype(_BF)


def kernel(x, meta_tokens, rel_bias, norm1_g, w_in, kv_norm_g, w_uk, w_uv, lambda_re, lambda_im, log_dt,
           b_re, b_im, c_re, c_im, d_skip, w_glu, gn_a, gn_b, gn_c, w_out, norm2_g, w_mlp1, w_mlp2, final_g):
    bsz, seq, _ = x.shape
    depth = w_in.shape[0]
    n_keys = seq + N_META
    k_top = min(TOPK_MAX, n_keys // 4)
    plen = n_keys + PAD
    rows = bsz * plen
    assert seq % ROW_TILE == 0 and rows % ROW_TILE == 0 and plen % S5_CHUNK == 0 and bsz % 8 == 0
    assert bsz % STEP_BATCH == 0

    head = jnp.concatenate([jnp.zeros((PAD, D_MODEL), x.dtype), meta_tokens.astype(x.dtype)], axis=0)
    h = jnp.concatenate([jnp.broadcast_to(head[None], (bsz, BLOCK, D_MODEL)), x], axis=1).reshape(rows, D_MODEL)

    bucket_t = jnp.swapaxes(_t5_bucket_tiles(), 1, 2)
    ar_ = jnp.arange(BLOCK)
    lower = (ar_[:, None] > ar_[None, :]).astype(_BF)
    ones = jnp.ones((BLOCK, BLOCK), _BF)
    low_ext = jnp.concatenate([lower, ones], axis=1)
    row1 = lambda v: v.reshape(1, -1).astype(_F32)

    for l in range(depth):
        w_all = _pack_w_in(w_in[l])
        wuk_bd = _block_diag(jnp.transpose(w_uk[l], (1, 2, 0))).astype(_BF)
        wuvt_pad = jnp.stack([jnp.pad(w_uv[l][:, hh, :].T, ((hh * HEAD_DIM, A_WIDTH - (hh + 1) * HEAD_DIM), (0, 0)))
                              for hh in range(A_HEADS)]).astype(_BF)
        qlat4, ckv, iq4, ikw, ub, qc, kc, vc = _inproj(h, row1(norm1_g[l]), w_all, row1(kv_norm_g[l]), wuk_bd)

        ckv3 = ckv.reshape(bsz, plen, KV_RANK)
        gk = PV_GROUP * BLOCK
        ckvt3 = jnp.swapaxes(jnp.pad(ckv3, ((0, 0), (0, -plen % gk), (0, 0))).reshape(bsz, -1, gk, KV_RANK), 2, 3)
        ya = _dsa(k_top, rel_bias.astype(_F32), qlat4, iq4, ikw.reshape(bsz, plen, 128),
                  ckv3, ckvt3, bucket_t, wuvt_pad, lower)

        wbu, a_pack, wc = _s5_params(lambda_re[l], lambda_im[l], log_dt[l], b_re[l], b_im[l], c_re[l], c_im[l])
        u_tb = jnp.swapaxes(ub.reshape(bsz, plen, B_WIDTH), 0, 1).reshape(rows, B_WIDTH)
        yb_tb = _s5(bsz, u_tb, wbu, a_pack, wc, row1(d_skip[l]), w_glu[l].astype(_BF))
        yb = jnp.swapaxes(yb_tb.reshape(plen, bsz, B_WIDTH), 0, 1).reshape(rows, B_WIDTH)

        yc = _stick(qc.reshape(bsz, plen, C_WIDTH), kc.reshape(bsz, plen, C_WIDTH),
                    vc.reshape(bsz, plen, C_WIDTH), low_ext)

        h = _mix_mlp(plen, l == depth - 1, h, ya, yb, yc, row1(gn_a[l]), row1(gn_b[l]), row1(gn_c[l]),
                     w_out[l].astype(_BF), row1(norm2_g[l]), w_mlp1[l].astype(_BF), w_mlp2[l].astype(_BF),
                     row1(final_g))

    return h.reshape(bsz, seq, D_MODEL)
```

```python
import functools
import math

import jax
import jax.numpy as jnp
import numpy as np
from jax import lax
from jax.experimental import pallas as pl
from jax.experimental.pallas import tpu as pltpu

D_MODEL = 1024
N_META = 16
BLOCK = 128
PAD = BLOCK - N_META
HEAD_DIM = 64
A_WIDTH = 256
A_HEADS = 4
KV_RANK = 128
IDX_HEADS = 4
IDX_DIM = 64
TOPK_MAX = 256
B_WIDTH = 512
S5_GROUP = 16
S5_GROUPS = 32
S5_STATE = 64
C_WIDTH = 256
C_HEADS = 4
D_FF = 4 * D_MODEL
NUM_BUCKETS = 32
MAX_DISTANCE = 128
RMS_EPS = 1e-6
NEG_INF = -1e30

ROW_TILE = 512
S5_CHUNK = 32
S5_LANES = S5_GROUPS * S5_STATE
S5_SLABS = 4
STEP_BATCH = 4
DSA_GROUP = 1
PV_GROUP = 4
BISECT_ITERS = 18
STICK_GROUP = 1
VMEM_LIMIT = 56 * 1024 * 1024

_BF = jnp.bfloat16
_F32 = jnp.float32

_C_QA = 0
_C_CKV = _C_QA + A_WIDTH
_C_IQ = _C_CKV + KV_RANK
_C_IKW = _C_IQ + IDX_HEADS * IDX_DIM
_C_UB = _C_IKW + 128
_C_QC = _C_UB + B_WIDTH
_C_KC = _C_QC + C_WIDTH
_C_VC = _C_KC + C_WIDTH
_N_PROJ = _C_VC + C_WIDTH


def _rms(x, g):
    return x * lax.rsqrt(jnp.mean(x * x, axis=-1, keepdims=True) + RMS_EPS) * g


def _inproj_kernel(h_ref, g1_ref, w_ref, kvg_ref, wuk_ref,
                   qlat_ref, ckv_ref, iq_ref, ikw_ref, ub_ref, qc_ref, kc_ref, vc_ref):
    hn = _rms(h_ref[...], g1_ref[...]).astype(_BF)
    proj = jnp.dot(hn, w_ref[...], preferred_element_type=_F32)
    qa = proj[:, _C_QA:_C_QA + A_WIDTH].astype(_BF)
    qlat = jnp.dot(qa, wuk_ref[...], preferred_element_type=_F32) * (HEAD_DIM ** -0.5)
    for hh in range(A_HEADS):
        qlat_ref[hh] = qlat[:, hh * KV_RANK:(hh + 1) * KV_RANK].astype(_BF)
    ckv_ref[...] = _rms(proj[:, _C_CKV:_C_CKV + KV_RANK], kvg_ref[...]).astype(_BF)
    for hh in range(IDX_HEADS):
        c0 = _C_IQ + hh * IDX_DIM
        iq_ref[hh] = proj[:, c0:c0 + IDX_DIM].astype(_BF)
    ikw_ref[...] = proj[:, _C_IKW:_C_IKW + 128]
    ub_ref[...] = proj[:, _C_UB:_C_UB + B_WIDTH].astype(_BF)
    qc_ref[...] = (proj[:, _C_QC:_C_QC + C_WIDTH] * (HEAD_DIM ** -0.5)).astype(_BF)
    kc_ref[...] = proj[:, _C_KC:_C_KC + C_WIDTH].astype(_BF)
    vc_ref[...] = proj[:, _C_VC:_C_VC + C_WIDTH].astype(_BF)


def _const_spec(shape):
    nd = len(shape)
    return pl.BlockSpec(shape, lambda *_: (0,) * nd, pipeline_mode=pl.Buffered(1))


def _inproj(h2d, g1, w_all, kvg, wuk_bd):
    rows = h2d.shape[0]
    tm = ROW_TILE
    row = lambda w: pl.BlockSpec((tm, w), lambda i: (i, 0))
    head = lambda w: pl.BlockSpec((4, tm, w), lambda i: (0, i, 0))
    out_shape = (
        jax.ShapeDtypeStruct((A_HEADS, rows, KV_RANK), _BF),
        jax.ShapeDtypeStruct((rows, KV_RANK), _BF),
        jax.ShapeDtypeStruct((IDX_HEADS, rows, IDX_DIM), _BF),
        jax.ShapeDtypeStruct((rows, 128), _F32),
        jax.ShapeDtypeStruct((rows, B_WIDTH), _BF),
        jax.ShapeDtypeStruct((rows, C_WIDTH), _BF),
        jax.ShapeDtypeStruct((rows, C_WIDTH), _BF),
        jax.ShapeDtypeStruct((rows, C_WIDTH), _BF),
    )
    return pl.pallas_call(
        _inproj_kernel,
        grid=(rows // tm,),
        in_specs=[row(D_MODEL), _const_spec((1, D_MODEL)), _const_spec((D_MODEL, _N_PROJ)),
                  _const_spec((1, KV_RANK)), _const_spec((A_WIDTH, A_HEADS * KV_RANK))],
        out_specs=(head(KV_RANK), row(KV_RANK), head(IDX_DIM), row(128), row(B_WIDTH),
                   row(C_WIDTH), row(C_WIDTH), row(C_WIDTH)),
        out_shape=out_shape,
        compiler_params=pltpu.CompilerParams(dimension_semantics=("arbitrary",),
                                             vmem_limit_bytes=VMEM_LIMIT),
        name="inproj",
    )(h2d, g1, w_all, kvg, wuk_bd)


def _dsa_kernel(k_top, nb, relb_ref, qlat_ref, iq_ref, ikw_ref, ckv_ref, ckvt_ref, bucket_ref, wuvt_ref,
                early_ref, ya_ref, score_ref, logit_ref, p_ref, acc_ref, bias_ref):
    grp = DSA_GROUP
    nbt = STEP_BATCH
    i = pl.program_id(1)
    ngrp = (i + grp) // grp
    kf = float(k_top)
    hw = A_HEADS * BLOCK
    bbs = range(nbt)

    @pl.when(i == 0)
    def _():
        for dd in range(2):
            bk = bucket_ref[dd]
            for hh in range(A_HEADS):
                acc = jnp.zeros((BLOCK, BLOCK), _F32)
                for b_ in range(NUM_BUCKETS):
                    acc = jnp.where(bk == b_, relb_ref[b_, hh], acc)
                bias_ref[dd, :, hh * BLOCK:(hh + 1) * BLOCK] = acc
        for hh in range(A_HEADS):
            bias_ref[2, :, hh * BLOCK:(hh + 1) * BLOCK] = jnp.full((BLOCK, BLOCK), relb_ref[NUM_BUCKETS - 1, hh], _F32)

    key_s = lax.broadcasted_iota(jnp.int32, (BLOCK, BLOCK), 0)
    q_t = i * BLOCK + lax.broadcasted_iota(jnp.int32, (BLOCK, BLOCK), 1)

    def blocks_of(t):
        return [grp * t + g for g in range(grp)]

    def rows_of(j):
        return pl.ds(pl.multiple_of(jnp.minimum(j, nb - 1) * BLOCK, BLOCK), BLOCK)

    def col_sum(x):
        return jnp.sum(x, axis=0, keepdims=True)

    def fold8(x, op):
        return op(x.reshape(BLOCK // 8, 8, BLOCK), axis=0)

    def row_any(flags):
        return functools.reduce(jnp.maximum, [jnp.max(f) for f in flags]) > 0.0

    iq = [iq_ref[:, bb].reshape(hw, IDX_DIM) for bb in bbs]
    q4 = [qlat_ref[:, bb].reshape(hw, KV_RANK) for bb in bbs]
    wrow = []
    for bb in bbs:
        wq_t = ikw_ref[bb, pl.ds(pl.multiple_of(i * BLOCK, BLOCK), BLOCK), :].T
        wrow.append([wq_t[IDX_DIM + hh:IDX_DIM + hh + 1, :] * (IDX_DIM ** -0.5 * IDX_HEADS ** -0.5)
                     for hh in range(IDX_HEADS)])
    nt = (((1,), (1,)), ((), ()))

    def score_body(t, carry):
        mn, mx = list(carry[0]), list(carry[1])
        js = blocks_of(t)
        work = [(bb, j) for bb in bbs for j in js]
        iks = [ikw_ref[bb, rows_of(j), :][:, :IDX_DIM].astype(_BF) for bb, j in work]
        ckvs = [ckv_ref[bb, rows_of(j), :] for bb, j in work]
        raws = [lax.dot_general(ik, iq[bb], nt, preferred_element_type=_F32) for ik, (bb, j) in zip(iks, work)]
        lts = [lax.dot_general(ckv, q4[bb], nt, preferred_element_type=_F32) for ckv, (bb, j) in zip(ckvs, work)]
        outs = []
        for (bb, j), raw in zip(work, raws):
            s = jnp.zeros((BLOCK, BLOCK), _F32)
            for hh in range(IDX_HEADS):
                s = s + jnp.maximum(raw[:, hh * BLOCK:(hh + 1) * BLOCK], 0.0) * wrow[bb][hh]
            kpos = j * BLOCK + key_s
            adm = (kpos <= q_t) & (kpos >= PAD)
            s_lo = jnp.where(adm, s, -jnp.inf)
            mn[bb] = jnp.minimum(mn[bb], fold8(jnp.where(adm, s, jnp.inf), jnp.min))
            mx[bb] = jnp.maximum(mx[bb], fold8(s_lo, jnp.max))
            outs.append(s_lo)
        for (bb, j), s_lo, lt in zip(work, outs, lts):
            score_ref[bb, j] = s_lo
            logit_ref[bb, j] = lt
        return mn, mx

    mn, mx = lax.fori_loop(0, ngrp, score_body, ([jnp.full((8, BLOCK), jnp.inf, _F32)] * nbt,
                                                 [jnp.full((8, BLOCK), -jnp.inf, _F32)] * nbt))
    s_min = [jnp.min(x, axis=0, keepdims=True) for x in mn]
    s_max = [jnp.max(x, axis=0, keepdims=True) for x in mx]

    n_adm = (i * BLOCK + lax.broadcasted_iota(jnp.int32, (1, BLOCK), 1) - (PAD - 1)).astype(_F32)
    take_all = n_adm <= kf

    def count_where(pred):
        def body(t, acc):
            acc = list(acc)
            for bb in bbs:
                for j in blocks_of(t):
                    acc[bb] = acc[bb] + fold8(jnp.where(pred(bb, score_ref[bb, j]), 1.0, 0.0), jnp.sum)
            return acc
        acc = lax.fori_loop(0, ngrp, body, [jnp.zeros((8, BLOCK), _F32)] * nbt)
        return [col_sum(a) for a in acc]

    def max_where(pred):
        def body(t, m):
            m = list(m)
            for bb in bbs:
                for j in blocks_of(t):
                    s = score_ref[bb, j]
                    m[bb] = jnp.maximum(m[bb], fold8(jnp.where(pred(bb, s), s, -jnp.inf), jnp.max))
            return m
        m = lax.fori_loop(0, ngrp, body, [jnp.full((8, BLOCK), -jnp.inf, _F32)] * nbt)
        return [jnp.max(x, axis=0, keepdims=True) for x in m]

    def select_threshold():
        def bis(_, st):
            lo, hi, c_hi = st
            mid = [(lo[bb] + hi[bb]) * 0.5 for bb in bbs]
            c = count_where(lambda bb, s: s > mid[bb])
            take = [c[bb] < kf for bb in bbs]
            return ([jnp.where(take[bb], lo[bb], mid[bb]) for bb in bbs],
                    [jnp.where(take[bb], mid[bb], hi[bb]) for bb in bbs],
                    [jnp.where(take[bb], c[bb], c_hi[bb]) for bb in bbs])

        _, hi_f, c_gt = lax.fori_loop(0, BISECT_ITERS, bis, (s_min, s_max, [jnp.zeros((1, BLOCK), _F32)] * nbt))
        thr = max_where(lambda bb, s: s <= hi_f[bb])
        g = count_where(lambda bb, s: s == thr[bb])

        def not_done(st):
            thr, c_gt, g, it = st
            pending = [jnp.where(take_all, 0.0, jnp.where(c_gt[bb] + g[bb] >= kf, 0.0, 1.0)) for bb in bbs]
            return jnp.logical_and(row_any(pending), it < 4096)

        def peel(st):
            thr, c_gt, g, it = st
            done = [jnp.logical_or(take_all, c_gt[bb] + g[bb] >= kf) for bb in bbs]
            thr_n = max_where(lambda bb, s: s < thr[bb])
            g_n = count_where(lambda bb, s: s == thr_n[bb])
            return ([jnp.where(done[bb], thr[bb], thr_n[bb]) for bb in bbs],
                    [jnp.where(done[bb], c_gt[bb], c_gt[bb] + g[bb]) for bb in bbs],
                    [jnp.where(done[bb], g[bb], g_n[bb]) for bb in bbs], it + 1)

        thr, c_gt, g, _ = lax.while_loop(not_done, peel, (thr, c_gt, g, jnp.int32(0)))
        return thr, c_gt, g

    zrow = jnp.zeros((1, BLOCK), _F32)
    thr, c_gt, g = lax.cond(i * BLOCK + (BLOCK - 1) - (PAD - 1) > k_top, select_threshold,
                            lambda: ([zrow] * nbt, [zrow] * nbt, [zrow] * nbt))
    thr = [jnp.where(take_all, float(jnp.finfo(jnp.float32).min), x) for x in thr]
    need = [kf - x for x in c_gt]
    surplus_ties = row_any([jnp.where(take_all, 0.0, jnp.where(c_gt[bb] + g[bb] > kf, 1.0, 0.0)) for bb in bbs])

    def mask_pass(rank_ties):
        def body(t, st):
            eq_before, m8 = list(st[0]), [list(x) for x in st[1]]
            js = blocks_of(t)
            work = [(bb, j) for bb in bbs for j in js]
            ss = [score_ref[bb, j] for bb, j in work]
            lts = [logit_ref[bb, j] for bb, j in work]
            biases = [bias_ref[jnp.clip(i - j, 0, 2)] for j in js]
            blks = []
            for n, (bb, j) in enumerate(work):
                s, lt, bias = ss[n], lts[n], biases[n % grp]
                if rank_ties:
                    eq = s == thr[bb]
                    eqf = jnp.where(eq, 1.0, 0.0)
                    rank = eq_before[bb] + jnp.dot(early_ref[...], eqf.astype(_BF), preferred_element_type=_F32)
                    sel = (s > thr[bb]) | (eq & (rank < need[bb]))
                    eq_before[bb] = eq_before[bb] + col_sum(eqf)
                else:
                    sel = s >= thr[bb]
                row = []
                for hh in range(A_HEADS):
                    hs = slice(hh * BLOCK, (hh + 1) * BLOCK)
                    blk = jnp.where(sel, lt[:, hs] + bias[:, hs], NEG_INF)
                    m8[bb][hh] = jnp.maximum(m8[bb][hh], fold8(blk, jnp.max))
                    row.append(blk)
                blks.append(row)
            for (bb, j), row in zip(work, blks):
                for hh in range(A_HEADS):
                    logit_ref[bb, j, :, hh * BLOCK:(hh + 1) * BLOCK] = row[hh]
            return eq_before, m8

        _, m8 = lax.fori_loop(0, ngrp, body,
                              ([zrow] * nbt, [[jnp.full((8, BLOCK), NEG_INF, _F32)] * A_HEADS] * nbt))
        return m8

    m8 = lax.cond(surplus_ties, lambda: mask_pass(True), lambda: mask_pass(False))
    m = [jnp.concatenate([jnp.max(x, axis=0, keepdims=True) for x in m8[bb]], axis=1) for bb in bbs]

    pgrp = PV_GROUP
    npg = (i + pgrp) // pgrp

    def pad_block(j, _):
        for bb in bbs:
            logit_ref[bb, j] = jnp.full((BLOCK, hw), NEG_INF, _F32)
        return 0

    lax.fori_loop(ngrp * grp, npg * pgrp, pad_block, 0)
    p_ref[1] = jnp.zeros(p_ref.shape[1:], _BF)
    acc_ref[...] = jnp.zeros_like(acc_ref)

    def pv_body(t, l8):
        slot = lax.rem(t, 2)
        l8 = [list(x) for x in l8]
        accs = [acc_ref[bb] + jnp.dot(ckvt_ref[bb, jnp.maximum(t - 1, 0)], p_ref[1 - slot, bb],
                                      preferred_element_type=_F32) for bb in bbs]
        ps = [[jnp.exp(logit_ref[bb, pgrp * t + g] - m[bb]) for g in range(pgrp)] for bb in bbs]
        for bb in bbs:
            for hh in range(A_HEADS):
                l8[bb][hh] = l8[bb][hh] + sum(fold8(p[:, hh * BLOCK:(hh + 1) * BLOCK], jnp.sum) for p in ps[bb])
        for bb in bbs:
            acc_ref[bb] = accs[bb]
            p_ref[slot, bb] = jnp.concatenate([p.astype(_BF) for p in ps[bb]], axis=0)
        return l8

    l8 = lax.fori_loop(0, npg, pv_body, [[jnp.zeros((8, BLOCK), _F32)] * A_HEADS] * nbt)
    for bb in bbs:
        acc = acc_ref[bb] + jnp.dot(ckvt_ref[bb, npg - 1], p_ref[lax.rem(npg - 1, 2), bb],
                                    preferred_element_type=_F32)
        l = jnp.concatenate([col_sum(x) for x in l8[bb]], axis=1)
        o_lat = (acc / l).astype(_BF)
        out_t = jnp.zeros((A_WIDTH, BLOCK), _F32)
        for hh in range(A_HEADS):
            out_t = out_t + jnp.dot(wuvt_ref[hh], o_lat[:, hh * BLOCK:(hh + 1) * BLOCK], preferred_element_type=_F32)
        ya_ref[bb] = out_t.T.astype(_BF)


def _dsa(k_top, rel_bias, qlat4, iq4, ikw3, ckv3, ckvt4, bucket_t, wuvt_pad, early):
    bsz, plen, _ = ckv3.shape
    nb = plen // BLOCK
    nbt = STEP_BATCH
    nblk = -(-nb // DSA_GROUP) * DSA_GROUP
    ngmax = -(-nb // PV_GROUP)
    nblk_pv = max(nblk, ngmax * PV_GROUP)
    gk = PV_GROUP * BLOCK
    hw = A_HEADS * BLOCK
    assert ckvt4.shape == (bsz, ngmax, KV_RANK, gk) and bsz % nbt == 0
    return pl.pallas_call(
        functools.partial(_dsa_kernel, k_top, nb),
        grid=(bsz // nbt, nb),
        in_specs=[
            pl.BlockSpec(memory_space=pltpu.SMEM),
            pl.BlockSpec((A_HEADS, nbt, BLOCK, KV_RANK), lambda b, i: (0, b, i, 0)),
            pl.BlockSpec((IDX_HEADS, nbt, BLOCK, IDX_DIM), lambda b, i: (0, b, i, 0)),
            pl.BlockSpec((nbt, plen, 128), lambda b, i: (b, 0, 0)),
            pl.BlockSpec((nbt, plen, KV_RANK), lambda b, i: (b, 0, 0)),
            pl.BlockSpec((nbt, ngmax, KV_RANK, gk), lambda b, i: (b, 0, 0, 0)),
            _const_spec((2, BLOCK, BLOCK)),
            _const_spec((A_HEADS, A_WIDTH, KV_RANK)),
            _const_spec((BLOCK, BLOCK)),
        ],
        out_specs=pl.BlockSpec((nbt, BLOCK, A_WIDTH), lambda b, i: (b, i, 0)),
        out_shape=jax.ShapeDtypeStruct((bsz, plen, A_WIDTH), _BF),
        scratch_shapes=[pltpu.VMEM((nbt, nblk, BLOCK, BLOCK), _F32),
                        pltpu.VMEM((nbt, nblk_pv, BLOCK, hw), _F32),
                        pltpu.VMEM((2, nbt, gk, hw), _BF),
                        pltpu.VMEM((nbt, KV_RANK, hw), _F32),
                        pltpu.VMEM((3, BLOCK, hw), _F32)],
        compiler_params=pltpu.CompilerParams(dimension_semantics=("arbitrary", "arbitrary"),
                                             vmem_limit_bytes=VMEM_LIMIT),
        name="dsa",
    )(rel_bias, qlat4.reshape(A_HEADS, bsz, plen, KV_RANK), iq4.reshape(IDX_HEADS, bsz, plen, IDX_DIM),
      ikw3, ckv3, ckvt4, bucket_t, wuvt_pad, early).reshape(bsz * plen, A_WIDTH)


def _stick_kernel(nb, q_ref, k_ref, v_ref, low_ref, yc_ref, z_ref, w_ref, carry_ref, acc_ref):
    grp = STICK_GROUP
    nbt = STEP_BATCH
    i = pl.program_id(1)
    ngrp = (i + grp) // grp
    row_t = i * BLOCK + lax.broadcasted_iota(jnp.int32, (BLOCK, BLOCK), 0)
    lane = lax.broadcasted_iota(jnp.int32, (BLOCK, BLOCK), 1)
    first_head = lane < HEAD_DIM
    seqs = [(bb, hh) for bb in range(nbt) for hh in range(C_HEADS)]
    nseq = len(seqs)

    qm = []
    for bb, hh in seqs:
        qp = q_ref[bb, :, (hh // 2) * BLOCK:(hh // 2 + 1) * BLOCK].astype(_F32)
        keep = first_head if hh % 2 == 0 else jnp.logical_not(first_head)
        qm.append(jnp.where(keep, qp, 0.0).astype(_BF))

    def blocks_of(t):
        base = (ngrp - 1 - t) * grp
        return [base + (grp - 1 - g) for g in range(grp)]

    def rows_of(j):
        return pl.ds(pl.multiple_of(jnp.clip(j, 0, nb - 1) * BLOCK, BLOCK), BLOCK)

    def pair_tiles(ref, j):
        return [[ref[bb, rows_of(j), pr * BLOCK:(pr + 1) * BLOCK] for pr in range(C_HEADS // 2)]
                for bb in range(nbt)]

    def qk(t):
        out = []
        for j in blocks_of(t):
            kps = pair_tiles(k_ref, j)
            out += [lax.dot_general(qm[n], kps[bb][hh // 2], (((1,), (1,)), ((), ())),
                                    preferred_element_type=_F32) for n, (bb, hh) in enumerate(seqs)]
        return out

    def wv(t, slot, acc):
        vs = [pair_tiles(v_ref, j) for j in blocks_of(t)]
        new = []
        for n, (bb, hh) in enumerate(seqs):
            wcat = jnp.concatenate([w_ref[slot, g * nseq + n] for g in range(grp)], axis=1)
            vcat = jnp.concatenate([vs[g][bb][hh // 2] for g in range(grp)], axis=0)
            new.append(acc[n] + jnp.dot(wcat, vcat, preferred_element_type=_F32))
        return new

    for n, z in enumerate(qk(0)):
        z_ref[0, n] = z
    w_ref[1] = jnp.zeros(w_ref.shape[1:], _BF)
    carry_ref[...] = jnp.zeros_like(carry_ref)
    acc_ref[...] = jnp.zeros_like(acc_ref)

    def trip(t, masked):
        slot = lax.rem(t, 2)
        acc = wv(t - 1, 1 - slot, [acc_ref[n] for n in range(nseq)])
        z_next = qk(t + 1)
        carry = [carry_ref[n] for n in range(nseq)]
        zs = [z_ref[slot, n] for n in range(grp * nseq)]

        lks, zzs, masks = [], [], []
        for g, j in enumerate(blocks_of(t)):
            kpos = j * BLOCK + lane
            strict = (kpos < row_t) & (kpos >= PAD)
            masks.append(strict)
            for n in range(nseq):
                z = zs[g * nseq + n]
                zz = jnp.minimum(z, 0.0) - jnp.log(1.0 + jnp.exp(-jnp.abs(z)))
                lks.append((jnp.where(strict, zz - z, 0.0) if masked else zz - z).astype(_BF))
                zzs.append(zz)
        cums = [jnp.dot(lk, low_ref[...], preferred_element_type=_F32) for lk in lks]
        ws = []
        for g in range(grp):
            for n in range(nseq):
                c = g * nseq + n
                w = jnp.exp(zzs[c] + cums[c][:, :BLOCK] + carry[n])
                ws.append((jnp.where(masks[g], w, 0.0) if masked else w).astype(_BF))
                carry[n] = carry[n] + cums[c][:, BLOCK:]

        for n in range(nseq):
            acc_ref[n] = acc[n]
            carry_ref[n] = carry[n]
        for c in range(grp * nseq):
            z_ref[1 - slot, c] = z_next[c]
            w_ref[slot, c] = ws[c]

    def plain_trip(t, _):
        trip(t, False)
        return 0

    trip(0, True)
    lax.fori_loop(1, ngrp - 1, plain_trip, 0)

    @pl.when(ngrp >= 2)
    def _():
        trip(ngrp - 1, True)

    acc = wv(ngrp - 1, lax.rem(ngrp - 1, 2), [acc_ref[n] for n in range(nseq)])
    for bb in range(nbt):
        yc_ref[bb] = jnp.concatenate([jnp.where(first_head, acc[bb * C_HEADS + 2 * pr], acc[bb * C_HEADS + 2 * pr + 1])
                                      for pr in range(C_HEADS // 2)], axis=1).astype(_BF)


def _stick(qc3, kc3, vc3, low_ext):
    bsz, plen, _ = kc3.shape
    nb = plen // BLOCK
    nbt = STEP_BATCH
    nseq = nbt * C_HEADS
    nchain = STICK_GROUP * nseq
    seq_blk = lambda rows: pl.BlockSpec((nbt, rows, C_WIDTH), (lambda b, i: (b, i, 0)) if rows == BLOCK
                                        else (lambda b, i: (b, 0, 0)))
    return pl.pallas_call(
        functools.partial(_stick_kernel, nb),
        grid=(bsz // nbt, nb),
        in_specs=[seq_blk(BLOCK), seq_blk(plen), seq_blk(plen), _const_spec((BLOCK, 2 * BLOCK))],
        out_specs=seq_blk(BLOCK),
        out_shape=jax.ShapeDtypeStruct((bsz, plen, C_WIDTH), _BF),
        scratch_shapes=[pltpu.VMEM((2, nchain, BLOCK, BLOCK), _F32),
                        pltpu.VMEM((2, nchain, BLOCK, BLOCK), _BF),
                        pltpu.VMEM((nseq, BLOCK, BLOCK), _F32),
                        pltpu.VMEM((nseq, BLOCK, BLOCK), _F32)],
        compiler_params=pltpu.CompilerParams(dimension_semantics=("arbitrary", "arbitrary"),
                                             vmem_limit_bytes=VMEM_LIMIT),
        name="stick",
    )(qc3, kc3, vc3, low_ext).reshape(bsz * plen, C_WIDTH)


def _s5_kernel(bsz, u_ref, wbu_ref, a_ref, wc_ref, d_ref, wglu_ref, y_ref, bur_ref, bui_ref, st_ref):
    c = pl.program_id(0)
    slab_w = S5_LANES // S5_SLABS
    ch_w = B_WIDTH // S5_SLABS

    @pl.when(c == 0)
    def _():
        st_ref[...] = jnp.zeros_like(st_ref)

    u = u_ref[...]
    ys = []
    for k in range(S5_SLABS):
        ls = slice(k * slab_w, (k + 1) * slab_w)
        o = jnp.dot(u[:, k * ch_w:(k + 1) * ch_w], wbu_ref[k], preferred_element_type=_F32)
        bur_ref[:, ls] = o[:, :slab_w]
        bui_ref[:, ls] = o[:, slab_w:]
        ar = jnp.broadcast_to(a_ref[0:1, ls], (bsz, slab_w))
        ai = jnp.broadcast_to(a_ref[1:2, ls], (bsz, slab_w))
        hr, hi = st_ref[0, :, ls], st_ref[1, :, ls]
        for t in range(S5_CHUNK):
            rows = slice(t * bsz, (t + 1) * bsz)
            hr, hi = (ar * hr - ai * hi + bur_ref[rows, ls], ar * hi + ai * hr + bui_ref[rows, ls])
            bur_ref[rows, ls] = hr
            bui_ref[rows, ls] = hi
        st_ref[0, :, ls] = hr
        st_ref[1, :, ls] = hi
        hcat = jnp.concatenate([bur_ref[:, ls].astype(_BF), bui_ref[:, ls].astype(_BF)], axis=1)
        ys.append(jnp.dot(hcat, wc_ref[k], preferred_element_type=_F32))
    y = jnp.concatenate(ys, axis=1) + d_ref[...] * u.astype(_F32)
    y = 0.5 * y * (1.0 + jnp.tanh(math.sqrt(2.0 / math.pi) * (y + 0.044715 * (y * y * y))))
    gate = jnp.dot(y.astype(_BF), wglu_ref[...], preferred_element_type=_F32)
    y_ref[...] = (y * (1.0 / (1.0 + jnp.exp(-gate)))).astype(_BF)


def _s5(bsz, u_tb, wbu, a_pack, wc, d_skip, w_glu):
    rows = u_tb.shape[0]
    rc = S5_CHUNK * bsz
    return pl.pallas_call(
        functools.partial(_s5_kernel, bsz),
        grid=(rows // rc,),
        in_specs=[
            pl.BlockSpec((rc, B_WIDTH), lambda c: (c, 0)),
            _const_spec((S5_SLABS, B_WIDTH // S5_SLABS, 2 * S5_LANES // S5_SLABS)),
            _const_spec((2, S5_LANES)),
            _const_spec((S5_SLABS, 2 * S5_LANES // S5_SLABS, B_WIDTH // S5_SLABS)),
            _const_spec((1, B_WIDTH)),
            _const_spec((B_WIDTH, B_WIDTH)),
        ],
        out_specs=pl.BlockSpec((rc, B_WIDTH), lambda c: (c, 0)),
        out_shape=jax.ShapeDtypeStruct((rows, B_WIDTH), _BF),
        scratch_shapes=[pltpu.VMEM((rc, S5_LANES), _F32), pltpu.VMEM((rc, S5_LANES), _F32),
                        pltpu.VMEM((2, bsz, S5_LANES), _F32)],
        compiler_params=pltpu.CompilerParams(dimension_semantics=("arbitrary",),
                                             vmem_limit_bytes=VMEM_LIMIT),
        name="s5",
    )(u_tb, wbu, a_pack, wc, d_skip, w_glu)


def _mix_mlp_kernel(plen, final, h_ref, ya_ref, yb_ref, yc_ref, ga_ref, gb_ref, gc_ref, wo_ref,
                    g2_ref, w1_ref, w2_ref, gf_ref, o_ref):
    tm = h_ref.shape[0]
    y = jnp.concatenate([_rms(ya_ref[...].astype(_F32), ga_ref[...]),
                         _rms(yb_ref[...].astype(_F32), gb_ref[...]),
                         _rms(yc_ref[...].astype(_F32), gc_ref[...])], axis=1).astype(_BF)
    h = h_ref[...] + jnp.dot(y, wo_ref[...], preferred_element_type=_F32)
    hn = _rms(h, g2_ref[...]).astype(_BF)
    ffc = D_FF // 4
    for cc in range(4):
        a = jnp.dot(hn, w1_ref[:, cc * ffc:(cc + 1) * ffc], preferred_element_type=_F32)
        a = jnp.square(jnp.maximum(a, 0.0)).astype(_BF)
        h = h + jnp.dot(a, w2_ref[cc * ffc:(cc + 1) * ffc, :], preferred_element_type=_F32)
    if final:
        o_ref[...] = _rms(h, gf_ref[...])
    else:
        r = (pl.program_id(0) * tm + lax.broadcasted_iota(jnp.int32, (tm, 1), 0)).astype(_F32)
        pos = r - jnp.floor((r + 0.5) * (1.0 / plen)) * plen
        o_ref[...] = jnp.where(pos >= PAD, h, 0.0)


def _mix_mlp(plen, final, h2d, ya, yb, yc, ga, gb, gc, wo, g2, w1, w2, gf):
    rows = h2d.shape[0]
    tm = ROW_TILE
    if final:
        seq = plen - BLOCK
        tiles = seq // tm
        grid = (rows // plen, tiles)
        row = lambda w: pl.BlockSpec((pl.Element(tm), pl.Element(w)),
                                     lambda b, j: (pl.multiple_of(b * plen + BLOCK + tm * j, BLOCK), 0))
        out_spec = pl.BlockSpec((tm, D_MODEL), lambda b, j: (b * tiles + j, 0))
        out_rows = (rows // plen) * seq
    else:
        grid = (rows // tm,)
        row = lambda w: pl.BlockSpec((tm, w), lambda i: (i, 0))
        out_spec, out_rows = row(D_MODEL), rows
    return pl.pallas_call(
        functools.partial(_mix_mlp_kernel, plen, final),
        grid=grid,
        in_specs=[row(D_MODEL), row(A_WIDTH), row(B_WIDTH), row(C_WIDTH),
                  _const_spec((1, A_WIDTH)), _const_spec((1, B_WIDTH)), _const_spec((1, C_WIDTH)),
                  _const_spec((D_MODEL, D_MODEL)), _const_spec((1, D_MODEL)),
                  _const_spec((D_MODEL, D_FF)), _const_spec((D_FF, D_MODEL)), _const_spec((1, D_MODEL))],
        out_specs=out_spec,
        out_shape=jax.ShapeDtypeStruct((out_rows, D_MODEL), _F32),
        compiler_params=pltpu.CompilerParams(dimension_semantics=("arbitrary",) * len(grid),
                                             vmem_limit_bytes=VMEM_LIMIT),
        name="mix_mlp",
    )(h2d, ya, yb, yc, ga, gb, gc, wo, g2, w1, w2, gf)


def _t5_bucket_tiles():
    max_exact = NUM_BUCKETS // 2
    r = jnp.arange(BLOCK)[:, None]
    c = jnp.arange(BLOCK)[None, :]
    tiles = []
    for dd in range(2):
        n = jnp.maximum(dd * BLOCK + r - c, 0)
        nf = jnp.maximum(n, max_exact).astype(_F32)
        large = max_exact + jnp.floor(jnp.log(nf / max_exact) / math.log(MAX_DISTANCE / max_exact)
                                      * (NUM_BUCKETS - max_exact)).astype(jnp.int32)
        large = jnp.minimum(large, NUM_BUCKETS - 1)
        tiles.append(jnp.where(n < max_exact, n, large))
    return jnp.stack(tiles).astype(jnp.int32)


def _pack_w_in(w_in):
    sizes = (A_WIDTH, KV_RANK, IDX_HEADS * IDX_DIM, IDX_DIM, IDX_HEADS, B_WIDTH, C_WIDTH, C_WIDTH, C_WIDTH)
    offs = np.cumsum((0,) + sizes)
    qa, ckv, iq, ik, iw, ub, qc, kc, vc = [w_in[:, offs[n]:offs[n + 1]] for n in range(len(sizes))]
    zeros = lambda n: jnp.zeros((w_in.shape[0], n), w_in.dtype)
    cols = [qa, ckv, iq, ik, iw, zeros(128 - IDX_DIM - IDX_HEADS), ub, qc, kc, vc]
    return jnp.concatenate(cols, axis=1).astype(_BF)


def _block_diag(blocks):
    n, r, c = blocks.shape
    eye = jnp.eye(n, dtype=blocks.dtype)
    return (eye[:, None, :, None] * blocks[:, :, None, :]).reshape(n * r, n * c)


def _s5_params(lam_re, lam_im, log_dt, b_re, b_im, c_re, c_im):
    dt = jnp.exp(log_dt)[:, None]
    mag = jnp.exp(lam_re * dt)
    ar = mag * jnp.cos(lam_im * dt)
    ai = mag * jnp.sin(lam_im * dt)
    den = lam_re * lam_re + lam_im * lam_im
    nr, ni = ar - 1.0, ai
    fr = (nr * lam_re + ni * lam_im) / den
    fi = (ni * lam_re - nr * lam_im) / den
    bfr = fr[:, :, None] * b_re - fi[:, :, None] * b_im
    bfi = fr[:, :, None] * b_im + fi[:, :, None] * b_re
    gps = S5_GROUPS // S5_SLABS
    wbu, wc = [], []
    for k in range(S5_SLABS):
        gs = slice(k * gps, (k + 1) * gps)
        wbu.append(jnp.concatenate([_block_diag(jnp.swapaxes(bfr[gs], 1, 2)),
                                    _block_diag(jnp.swapaxes(bfi[gs], 1, 2))], axis=1))
        wc.append(jnp.concatenate([_block_diag(jnp.swapaxes(c_re[gs], 1, 2)),
                                   _block_diag(jnp.swapaxes(-c_im[gs], 1, 2))], axis=0))
    a_pack = jnp.stack([ar.reshape(-1), ai.reshape(-1)])
    return jnp.stack(wbu).astype(_BF), a_pack, jnp.stack(wc).astype(_BF)


def kernel(x, meta_tokens, rel_bias, norm1_g, w_in, kv_norm_g, w_uk, w_uv, lambda_re, lambda_im, log_dt,
           b_re, b_im, c_re, c_im, d_skip, w_glu, gn_a, gn_b, gn_c, w_out, norm2_g, w_mlp1, w_mlp2, final_g):
    bsz, seq, _ = x.shape
    depth = w_in.shape[0]
    n_keys = seq + N_META
    k_top = min(TOPK_MAX, n_keys // 4)
    plen = n_keys + PAD
    rows = bsz * plen
    assert seq % ROW_TILE == 0 and rows % ROW_TILE == 0 and plen % S5_CHUNK == 0 and bsz % 8 == 0
    assert bsz % STEP_BATCH == 0

    head = jnp.concatenate([jnp.zeros((PAD, D_MODEL), x.dtype), meta_tokens.astype(x.dtype)], axis=0)
    h = jnp.concatenate([jnp.broadcast_to(head[None], (bsz, BLOCK, D_MODEL)), x], axis=1).reshape(rows, D_MODEL)

    bucket_t = jnp.swapaxes(_t5_bucket_tiles(), 1, 2)
    ar_ = jnp.arange(BLOCK)
    lower = (ar_[:, None] > ar_[None, :]).astype(_BF)
    ones = jnp.ones((BLOCK, BLOCK), _BF)
    low_ext = jnp.concatenate([lower, ones], axis=1)
    row1 = lambda v: v.reshape(1, -1).astype(_F32)

    for l in range(depth):
        w_all = _pack_w_in(w_in[l])
        wuk_bd = _block_diag(jnp.transpose(w_uk[l], (1, 2, 0))).astype(_BF)
        wuvt_pad = jnp.stack([jnp.pad(w_uv[l][:, hh, :].T, ((hh * HEAD_DIM, A_WIDTH - (hh + 1) * HEAD_DIM), (0, 0)))
                              for hh in range(A_HEADS)]).astype(_BF)
        qlat4, ckv, iq4, ikw, ub, qc, kc, vc = _inproj(h, row1(norm1_g[l]), w_all, row1(kv_norm_g[l]), wuk_bd)

        ckv3 = ckv.reshape(bsz, plen, KV_RANK)
        gk = PV_GROUP * BLOCK
        ckvt3 = jnp.swapaxes(jnp.pad(ckv3, ((0, 0), (0, -plen % gk), (0, 0))).reshape(bsz, -1, gk, KV_RANK), 2, 3)
        ya = _dsa(k_top, rel_bias.astype(_F32), qlat4, iq4, ikw.reshape(bsz, plen, 128),
                  ckv3, ckvt3, bucket_t, wuvt_pad, lower)

        wbu, a_pack, wc = _s5_params(lambda_re[l], lambda_im[l], log_dt[l], b_re[l], b_im[l], c_re[l], c_im[l])
        u_tb = jnp.swapaxes(ub.reshape(bsz, plen, B_WIDTH), 0, 1).reshape(rows, B_WIDTH)
        yb_tb = _s5(bsz, u_tb, wbu, a_pack, wc, row1(d_skip[l]), w_glu[l].astype(_BF))
        yb = jnp.swapaxes(yb_tb.reshape(plen, bsz, B_WIDTH), 0, 1).reshape(rows, B_WIDTH)

        yc = _stick(qc.reshape(bsz, plen, C_WIDTH), kc.reshape(bsz, plen, C_WIDTH),
                    vc.reshape(bsz, plen, C_WIDTH), low_ext)

        h = _mix_mlp(plen, l == depth - 1, h, ya, yb, yc, row1(gn_a[l]), row1(gn_b[l]), row1(gn_c[l]),
                     w_out[l].astype(_BF), row1(norm2_g[l]), w_mlp1[l].astype(_BF), w_mlp2[l].astype(_BF),
                     row1(final_g))

    return h.reshape(bsz, seq, D_MODEL)
```

```python
import functools
import math

import jax
import jax.numpy as jnp
import numpy as np
from jax import lax
from jax.experimental import pallas as pl
from jax.experimental.pallas import tpu as pltpu

D_MODEL = 1024
N_META = 16
BLOCK = 128
PAD = BLOCK - N_META
HEAD_DIM = 64
A_WIDTH = 256
A_HEADS = 4
KV_RANK = 128
IDX_HEADS = 4
IDX_DIM = 64
TOPK_MAX = 256
B_WIDTH = 512
S5_GROUP = 16
S5_GROUPS = 32
S5_STATE = 64
C_WIDTH = 256
C_HEADS = 4
D_FF = 4 * D_MODEL
NUM_BUCKETS = 32
MAX_DISTANCE = 128
RMS_EPS = 1e-6
NEG_INF = -1e30

ROW_TILE = 512
S5_CHUNK = 32
S5_LANES = S5_GROUPS * S5_STATE
S5_SLABS = 4
STEP_BATCH = 4
DSA_GROUP = 1
PV_GROUP = 4
BISECT_ITERS = 17
STICK_GROUP = 1
VMEM_LIMIT = 56 * 1024 * 1024

_BF = jnp.bfloat16
_F32 = jnp.float32

_C_QA = 0
_C_CKV = _C_QA + A_WIDTH
_C_IQ = _C_CKV + KV_RANK
_C_IKW = _C_IQ + IDX_HEADS * IDX_DIM
_C_UB = _C_IKW + 128
_C_QC = _C_UB + B_WIDTH
_C_KC = _C_QC + C_WIDTH
_C_VC = _C_KC + C_WIDTH
_N_PROJ = _C_VC + C_WIDTH


def _rms(x, g):
    return x * lax.rsqrt(jnp.mean(x * x, axis=-1, keepdims=True) + RMS_EPS) * g


def _inproj_kernel(h_ref, g1_ref, w_ref, kvg_ref, wuk_ref,
                   qlat_ref, ckv_ref, iq_ref, ikw_ref, ub_ref, qc_ref, kc_ref, vc_ref):
    hn = _rms(h_ref[...], g1_ref[...]).astype(_BF)
    proj = jnp.dot(hn, w_ref[...], preferred_element_type=_F32)
    qa = proj[:, _C_QA:_C_QA + A_WIDTH].astype(_BF)
    qlat = jnp.dot(qa, wuk_ref[...], preferred_element_type=_F32) * (HEAD_DIM ** -0.5)
    for hh in range(A_HEADS):
        qlat_ref[hh] = qlat[:, hh * KV_RANK:(hh + 1) * KV_RANK].astype(_BF)
    ckv_ref[...] = _rms(proj[:, _C_CKV:_C_CKV + KV_RANK], kvg_ref[...]).astype(_BF)
    for hh in range(IDX_HEADS):
        c0 = _C_IQ + hh * IDX_DIM
        iq_ref[hh] = proj[:, c0:c0 + IDX_DIM].astype(_BF)
    ikw_ref[...] = proj[:, _C_IKW:_C_IKW + 128]
    ub_ref[...] = proj[:, _C_UB:_C_UB + B_WIDTH].astype(_BF)
    qc_ref[...] = (proj[:, _C_QC:_C_QC + C_WIDTH] * (HEAD_DIM ** -0.5)).astype(_BF)
    kc_ref[...] = proj[:, _C_KC:_C_KC + C_WIDTH].astype(_BF)
    vc_ref[...] = proj[:, _C_VC:_C_VC + C_WIDTH].astype(_BF)


def _const_spec(shape):
    nd = len(shape)
    return pl.BlockSpec(shape, lambda *_: (0,) * nd, pipeline_mode=pl.Buffered(1))


def _inproj(h2d, g1, w_all, kvg, wuk_bd):
    rows = h2d.shape[0]
    tm = ROW_TILE
    row = lambda w: pl.BlockSpec((tm, w), lambda i: (i, 0))
    head = lambda w: pl.BlockSpec((4, tm, w), lambda i: (0, i, 0))
    out_shape = (
        jax.ShapeDtypeStruct((A_HEADS, rows, KV_RANK), _BF),
        jax.ShapeDtypeStruct((rows, KV_RANK), _BF),
        jax.ShapeDtypeStruct((IDX_HEADS, rows, IDX_DIM), _BF),
        jax.ShapeDtypeStruct((rows, 128), _F32),
        jax.ShapeDtypeStruct((rows, B_WIDTH), _BF),
        jax.ShapeDtypeStruct((rows, C_WIDTH), _BF),
        jax.ShapeDtypeStruct((rows, C_WIDTH), _BF),
        jax.ShapeDtypeStruct((rows, C_WIDTH), _BF),
    )
    return pl.pallas_call(
        _inproj_kernel,
        grid=(rows // tm,),
        in_specs=[row(D_MODEL), _const_spec((1, D_MODEL)), _const_spec((D_MODEL, _N_PROJ)),
                  _const_spec((1, KV_RANK)), _const_spec((A_WIDTH, A_HEADS * KV_RANK))],
        out_specs=(head(KV_RANK), row(KV_RANK), head(IDX_DIM), row(128), row(B_WIDTH),
                   row(C_WIDTH), row(C_WIDTH), row(C_WIDTH)),
        out_shape=out_shape,
        compiler_params=pltpu.CompilerParams(dimension_semantics=("arbitrary",),
                                             vmem_limit_bytes=VMEM_LIMIT),
        name="inproj",
    )(h2d, g1, w_all, kvg, wuk_bd)


def _dsa_kernel(k_top, nb, relb_ref, qlat_ref, iq_ref, ikw_ref, ckv_ref, ckvt_ref, bucket_ref, wuvt_ref,
                early_ref, ya_ref, score_ref, logit_ref, p_ref, acc_ref, bias_ref):
    grp = DSA_GROUP
    nbt = STEP_BATCH
    i = pl.program_id(1)
    ngrp = (i + grp) // grp
    kf = float(k_top)
    hw = A_HEADS * BLOCK
    bbs = range(nbt)

    @pl.when(i == 0)
    def _():
        for dd in range(2):
            bk = bucket_ref[dd]
            for hh in range(A_HEADS):
                acc = jnp.zeros((BLOCK, BLOCK), _F32)
                for b_ in range(NUM_BUCKETS):
                    acc = jnp.where(bk == b_, relb_ref[b_, hh], acc)
                bias_ref[dd, :, hh * BLOCK:(hh + 1) * BLOCK] = acc
        for hh in range(A_HEADS):
            bias_ref[2, :, hh * BLOCK:(hh + 1) * BLOCK] = jnp.full((BLOCK, BLOCK), relb_ref[NUM_BUCKETS - 1, hh], _F32)

    key_s = lax.broadcasted_iota(jnp.int32, (BLOCK, BLOCK), 0)
    q_t = i * BLOCK + lax.broadcasted_iota(jnp.int32, (BLOCK, BLOCK), 1)

    def blocks_of(t):
        return [grp * t + g for g in range(grp)]

    def rows_of(j):
        return pl.ds(pl.multiple_of(jnp.minimum(j, nb - 1) * BLOCK, BLOCK), BLOCK)

    def col_sum(x):
        return jnp.sum(x, axis=0, keepdims=True)

    def fold8(x, op):
        return op(x.reshape(BLOCK // 8, 8, BLOCK), axis=0)

    def row_any(flags):
        return functools.reduce(jnp.maximum, [jnp.max(f) for f in flags]) > 0.0

    iq = [iq_ref[:, bb].reshape(hw, IDX_DIM) for bb in bbs]
    q4 = [qlat_ref[:, bb].reshape(hw, KV_RANK) for bb in bbs]
    wrow = []
    for bb in bbs:
        wq_t = ikw_ref[bb, pl.ds(pl.multiple_of(i * BLOCK, BLOCK), BLOCK), :].T
        wrow.append([wq_t[IDX_DIM + hh:IDX_DIM + hh + 1, :] * (IDX_DIM ** -0.5 * IDX_HEADS ** -0.5)
                     for hh in range(IDX_HEADS)])
    nt = (((1,), (1,)), ((), ()))

    def score_body(t, carry):
        mn, mx = list(carry[0]), list(carry[1])
        js = blocks_of(t)
        work = [(bb, j) for bb in bbs for j in js]
        iks = [ikw_ref[bb, rows_of(j), :][:, :IDX_DIM].astype(_BF) for bb, j in work]
        ckvs = [ckv_ref[bb, rows_of(j), :] for bb, j in work]
        raws = [lax.dot_general(ik, iq[bb], nt, preferred_element_type=_F32) for ik, (bb, j) in zip(iks, work)]
        lts = [lax.dot_general(ckv, q4[bb], nt, preferred_element_type=_F32) for ckv, (bb, j) in zip(ckvs, work)]
        outs = []
        for (bb, j), raw in zip(work, raws):
            s = jnp.zeros((BLOCK, BLOCK), _F32)
            for hh in range(IDX_HEADS):
                s = s + jnp.maximum(raw[:, hh * BLOCK:(hh + 1) * BLOCK], 0.0) * wrow[bb][hh]
            kpos = j * BLOCK + key_s
            adm = (kpos <= q_t) & (kpos >= PAD)
            s_lo = jnp.where(adm, s, -jnp.inf)
            mn[bb] = jnp.minimum(mn[bb], fold8(jnp.where(adm, s, jnp.inf), jnp.min))
            mx[bb] = jnp.maximum(mx[bb], fold8(s_lo, jnp.max))
            outs.append(s_lo)
        for (bb, j), s_lo, lt in zip(work, outs, lts):
            score_ref[bb, j] = s_lo
            logit_ref[bb, j] = lt
        return mn, mx

    mn, mx = lax.fori_loop(0, ngrp, score_body, ([jnp.full((8, BLOCK), jnp.inf, _F32)] * nbt,
                                                 [jnp.full((8, BLOCK), -jnp.inf, _F32)] * nbt))
    s_min = [jnp.min(x, axis=0, keepdims=True) for x in mn]
    s_max = [jnp.max(x, axis=0, keepdims=True) for x in mx]

    n_adm = (i * BLOCK + lax.broadcasted_iota(jnp.int32, (1, BLOCK), 1) - (PAD - 1)).astype(_F32)
    take_all = n_adm <= kf

    def count_where(pred):
        def body(t, acc):
            acc = list(acc)
            for bb in bbs:
                for j in blocks_of(t):
                    acc[bb] = acc[bb] + fold8(jnp.where(pred(bb, score_ref[bb, j]), 1.0, 0.0), jnp.sum)
            return acc
        acc = lax.fori_loop(0, ngrp, body, [jnp.zeros((8, BLOCK), _F32)] * nbt)
        return [col_sum(a) for a in acc]

    def max_where(pred):
        def body(t, m):
            m = list(m)
            for bb in bbs:
                for j in blocks_of(t):
                    s = score_ref[bb, j]
                    m[bb] = jnp.maximum(m[bb], fold8(jnp.where(pred(bb, s), s, -jnp.inf), jnp.max))
            return m
        m = lax.fori_loop(0, ngrp, body, [jnp.full((8, BLOCK), -jnp.inf, _F32)] * nbt)
        return [jnp.max(x, axis=0, keepdims=True) for x in m]

    def select_threshold():
        def bis(_, st):
            lo, hi, c_hi = st
            mid = [(lo[bb] + hi[bb]) * 0.5 for bb in bbs]
            c = count_where(lambda bb, s: s > mid[bb])
            take = [c[bb] < kf for bb in bbs]
            return ([jnp.where(take[bb], lo[bb], mid[bb]) for bb in bbs],
                    [jnp.where(take[bb], mid[bb], hi[bb]) for bb in bbs],
                    [jnp.where(take[bb], c[bb], c_hi[bb]) for bb in bbs])

        _, hi_f, c_gt = lax.fori_loop(0, BISECT_ITERS, bis, (s_min, s_max, [jnp.zeros((1, BLOCK), _F32)] * nbt))
        thr = max_where(lambda bb, s: s <= hi_f[bb])
        g = count_where(lambda bb, s: s == thr[bb])

        def not_done(st):
            thr, c_gt, g, it = st
            pending = [jnp.where(take_all, 0.0, jnp.where(c_gt[bb] + g[bb] >= kf, 0.0, 1.0)) for bb in bbs]
            return jnp.logical_and(row_any(pending), it < 4096)

        def peel(st):
            thr, c_gt, g, it = st
            done = [jnp.logical_or(take_all, c_gt[bb] + g[bb] >= kf) for bb in bbs]
            thr_n = max_where(lambda bb, s: s < thr[bb])
            g_n = count_where(lambda bb, s: s == thr_n[bb])
            return ([jnp.where(done[bb], thr[bb], thr_n[bb]) for bb in bbs],
                    [jnp.where(done[bb], c_gt[bb], c_gt[bb] + g[bb]) for bb in bbs],
                    [jnp.where(done[bb], g[bb], g_n[bb]) for bb in bbs], it + 1)

        thr, c_gt, g, _ = lax.while_loop(not_done, peel, (thr, c_gt, g, jnp.int32(0)))
        return thr, c_gt, g

    zrow = jnp.zeros((1, BLOCK), _F32)
    thr, c_gt, g = lax.cond(i * BLOCK + (BLOCK - 1) - (PAD - 1) > k_top, select_threshold,
                            lambda: ([zrow] * nbt, [zrow] * nbt, [zrow] * nbt))
    thr = [jnp.where(take_all, float(jnp.finfo(jnp.float32).min), x) for x in thr]
    need = [kf - x for x in c_gt]
    surplus_ties = row_any([jnp.where(take_all, 0.0, jnp.where(c_gt[bb] + g[bb] > kf, 1.0, 0.0)) for bb in bbs])

    def mask_pass(rank_ties):
        def body(t, st):
            eq_before, m8 = list(st[0]), [list(x) for x in st[1]]
            js = blocks_of(t)
            work = [(bb, j) for bb in bbs for j in js]
            ss = [score_ref[bb, j] for bb, j in work]
            lts = [logit_ref[bb, j] for bb, j in work]
            biases = [bias_ref[jnp.clip(i - j, 0, 2)] for j in js]
            blks = []
            for n, (bb, j) in enumerate(work):
                s, lt, bias = ss[n], lts[n], biases[n % grp]
                if rank_ties:
                    eq = s == thr[bb]
                    eqf = jnp.where(eq, 1.0, 0.0)
                    rank = eq_before[bb] + jnp.dot(early_ref[...], eqf.astype(_BF), preferred_element_type=_F32)
                    sel = (s > thr[bb]) | (eq & (rank < need[bb]))
                    eq_before[bb] = eq_before[bb] + col_sum(eqf)
                else:
                    sel = s >= thr[bb]
                row = []
                for hh in range(A_HEADS):
                    hs = slice(hh * BLOCK, (hh + 1) * BLOCK)
                    blk = jnp.where(sel, lt[:, hs] + bias[:, hs], NEG_INF)
                    m8[bb][hh] = jnp.maximum(m8[bb][hh], fold8(blk, jnp.max))
                    row.append(blk)
                blks.append(row)
            for (bb, j), row in zip(work, blks):
                for hh in range(A_HEADS):
                    logit_ref[bb, j, :, hh * BLOCK:(hh + 1) * BLOCK] = row[hh]
            return eq_before, m8

        _, m8 = lax.fori_loop(0, ngrp, body,
                              ([zrow] * nbt, [[jnp.full((8, BLOCK), NEG_INF, _F32)] * A_HEADS] * nbt))
        return m8

    m8 = lax.cond(surplus_ties, lambda: mask_pass(True), lambda: mask_pass(False))
    m = [jnp.concatenate([jnp.max(x, axis=0, keepdims=True) for x in m8[bb]], axis=1) for bb in bbs]

    pgrp = PV_GROUP
    npg = (i + pgrp) // pgrp

    def pad_block(j, _):
        for bb in bbs:
            logit_ref[bb, j] = jnp.full((BLOCK, hw), NEG_INF, _F32)
        return 0

    lax.fori_loop(ngrp * grp, npg * pgrp, pad_block, 0)
    p_ref[1] = jnp.zeros(p_ref.shape[1:], _BF)
    acc_ref[...] = jnp.zeros_like(acc_ref)

    def pv_body(t, l8):
        slot = lax.rem(t, 2)
        l8 = [list(x) for x in l8]
        accs = [acc_ref[bb] + jnp.dot(ckvt_ref[bb, jnp.maximum(t - 1, 0)], p_ref[1 - slot, bb],
                                      preferred_element_type=_F32) for bb in bbs]
        ps = [[jnp.exp(logit_ref[bb, pgrp * t + g] - m[bb]) for g in range(pgrp)] for bb in bbs]
        for bb in bbs:
            for hh in range(A_HEADS):
                l8[bb][hh] = l8[bb][hh] + sum(fold8(p[:, hh * BLOCK:(hh + 1) * BLOCK], jnp.sum) for p in ps[bb])
        for bb in bbs:
            acc_ref[bb] = accs[bb]
            p_ref[slot, bb] = jnp.concatenate([p.astype(_BF) for p in ps[bb]], axis=0)
        return l8

    l8 = lax.fori_loop(0, npg, pv_body, [[jnp.zeros((8, BLOCK), _F32)] * A_HEADS] * nbt)
    for bb in bbs:
        acc = acc_ref[bb] + jnp.dot(ckvt_ref[bb, npg - 1], p_ref[lax.rem(npg - 1, 2), bb],
                                    preferred_element_type=_F32)
        l = jnp.concatenate([col_sum(x) for x in l8[bb]], axis=1)
        o_lat = (acc / l).astype(_BF)
        out_t = jnp.zeros((A_WIDTH, BLOCK), _F32)
        for hh in range(A_HEADS):
            out_t = out_t + jnp.dot(wuvt_ref[hh], o_lat[:, hh * BLOCK:(hh + 1) * BLOCK], preferred_element_type=_F32)
        ya_ref[bb] = out_t.T.astype(_BF)


def _dsa(k_top, rel_bias, qlat4, iq4, ikw3, ckv3, ckvt4, bucket_t, wuvt_pad, early):
    bsz, plen, _ = ckv3.shape
    nb = plen // BLOCK
    nbt = STEP_BATCH
    nblk = -(-nb // DSA_GROUP) * DSA_GROUP
    ngmax = -(-nb // PV_GROUP)
    nblk_pv = max(nblk, ngmax * PV_GROUP)
    gk = PV_GROUP * BLOCK
    hw = A_HEADS * BLOCK
    assert ckvt4.shape == (bsz, ngmax, KV_RANK, gk) and bsz % nbt == 0
    return pl.pallas_call(
        functools.partial(_dsa_kernel, k_top, nb),
        grid=(bsz // nbt, nb),
        in_specs=[
            pl.BlockSpec(memory_space=pltpu.SMEM),
            pl.BlockSpec((A_HEADS, nbt, BLOCK, KV_RANK), lambda b, i: (0, b, i, 0)),
            pl.BlockSpec((IDX_HEADS, nbt, BLOCK, IDX_DIM), lambda b, i: (0, b, i, 0)),
            pl.BlockSpec((nbt, plen, 128), lambda b, i: (b, 0, 0)),
            pl.BlockSpec((nbt, plen, KV_RANK), lambda b, i: (b, 0, 0)),
            pl.BlockSpec((nbt, ngmax, KV_RANK, gk), lambda b, i: (b, 0, 0, 0)),
            _const_spec((2, BLOCK, BLOCK)),
            _const_spec((A_HEADS, A_WIDTH, KV_RANK)),
            _const_spec((BLOCK, BLOCK)),
        ],
        out_specs=pl.BlockSpec((nbt, BLOCK, A_WIDTH), lambda b, i: (b, i, 0)),
        out_shape=jax.ShapeDtypeStruct((bsz, plen, A_WIDTH), _BF),
        scratch_shapes=[pltpu.VMEM((nbt, nblk, BLOCK, BLOCK), _F32),
                        pltpu.VMEM((nbt, nblk_pv, BLOCK, hw), _F32),
                        pltpu.VMEM((2, nbt, gk, hw), _BF),
                        pltpu.VMEM((nbt, KV_RANK, hw), _F32),
                        pltpu.VMEM((3, BLOCK, hw), _F32)],
        compiler_params=pltpu.CompilerParams(dimension_semantics=("arbitrary", "arbitrary"),
                                             vmem_limit_bytes=VMEM_LIMIT),
        name="dsa",
    )(rel_bias, qlat4.reshape(A_HEADS, bsz, plen, KV_RANK), iq4.reshape(IDX_HEADS, bsz, plen, IDX_DIM),
      ikw3, ckv3, ckvt4, bucket_t, wuvt_pad, early).reshape(bsz * plen, A_WIDTH)


def _stick_kernel(nb, q_ref, k_ref, v_ref, low_ref, yc_ref, z_ref, w_ref, carry_ref, acc_ref):
    grp = STICK_GROUP
    nbt = STEP_BATCH
    i = pl.program_id(1)
    ngrp = (i + grp) // grp
    row_t = i * BLOCK + lax.broadcasted_iota(jnp.int32, (BLOCK, BLOCK), 0)
    lane = lax.broadcasted_iota(jnp.int32, (BLOCK, BLOCK), 1)
    first_head = lane < HEAD_DIM
    seqs = [(bb, hh) for bb in range(nbt) for hh in range(C_HEADS)]
    nseq = len(seqs)

    qm = []
    for bb, hh in seqs:
        qp = q_ref[bb, :, (hh // 2) * BLOCK:(hh // 2 + 1) * BLOCK].astype(_F32)
        keep = first_head if hh % 2 == 0 else jnp.logical_not(first_head)
        qm.append(jnp.where(keep, qp, 0.0).astype(_BF))

    def blocks_of(t):
        base = (ngrp - 1 - t) * grp
        return [base + (grp - 1 - g) for g in range(grp)]

    def rows_of(j):
        return pl.ds(pl.multiple_of(jnp.clip(j, 0, nb - 1) * BLOCK, BLOCK), BLOCK)

    def pair_tiles(ref, j):
        return [[ref[bb, rows_of(j), pr * BLOCK:(pr + 1) * BLOCK] for pr in range(C_HEADS // 2)]
                for bb in range(nbt)]

    def qk(t):
        out = []
        for j in blocks_of(t):
            kps = pair_tiles(k_ref, j)
            out += [lax.dot_general(qm[n], kps[bb][hh // 2], (((1,), (1,)), ((), ())),
                                    preferred_element_type=_F32) for n, (bb, hh) in enumerate(seqs)]
        return out

    def wv(t, slot, acc):
        vs = [pair_tiles(v_ref, j) for j in blocks_of(t)]
        new = []
        for n, (bb, hh) in enumerate(seqs):
            wcat = jnp.concatenate([w_ref[slot, g * nseq + n] for g in range(grp)], axis=1)
            vcat = jnp.concatenate([vs[g][bb][hh // 2] for g in range(grp)], axis=0)
            new.append(acc[n] + jnp.dot(wcat, vcat, preferred_element_type=_F32))
        return new

    for n, z in enumerate(qk(0)):
        z_ref[0, n] = z
    w_ref[1] = jnp.zeros(w_ref.shape[1:], _BF)
    carry_ref[...] = jnp.zeros_like(carry_ref)
    acc_ref[...] = jnp.zeros_like(acc_ref)

    def trip(t, masked):
        slot = lax.rem(t, 2)
        acc = wv(t - 1, 1 - slot, [acc_ref[n] for n in range(nseq)])
        z_next = qk(t + 1)
        carry = [carry_ref[n] for n in range(nseq)]
        zs = [z_ref[slot, n] for n in range(grp * nseq)]

        lks, zzs, masks = [], [], []
        for g, j in enumerate(blocks_of(t)):
            kpos = j * BLOCK + lane
            strict = (kpos < row_t) & (kpos >= PAD)
            masks.append(strict)
            for n in range(nseq):
                z = zs[g * nseq + n]
                zz = jnp.minimum(z, 0.0) - jnp.log(1.0 + jnp.exp(-jnp.abs(z)))
                lks.append((jnp.where(strict, zz - z, 0.0) if masked else zz - z).astype(_BF))
                zzs.append(zz)
        cums = [jnp.dot(lk, low_ref[...], preferred_element_type=_F32) for lk in lks]
        ws = []
        for g in range(grp):
            for n in range(nseq):
                c = g * nseq + n
                w = jnp.exp(zzs[c] + cums[c][:, :BLOCK] + carry[n])
                ws.append((jnp.where(masks[g], w, 0.0) if masked else w).astype(_BF))
                carry[n] = carry[n] + cums[c][:, BLOCK:]

        for n in range(nseq):
            acc_ref[n] = acc[n]
            carry_ref[n] = carry[n]
        for c in range(grp * nseq):
            z_ref[1 - slot, c] = z_next[c]
            w_ref[slot, c] = ws[c]

    def plain_trip(t, _):
        trip(t, False)
        return 0

    trip(0, True)
    lax.fori_loop(1, ngrp - 1, plain_trip, 0)

    @pl.when(ngrp >= 2)
    def _():
        trip(ngrp - 1, True)

    acc = wv(ngrp - 1, lax.rem(ngrp - 1, 2), [acc_ref[n] for n in range(nseq)])
    for bb in range(nbt):
        yc_ref[bb] = jnp.concatenate([jnp.where(first_head, acc[bb * C_HEADS + 2 * pr], acc[bb * C_HEADS + 2 * pr + 1])
                                      for pr in range(C_HEADS // 2)], axis=1).astype(_BF)


def _stick(qc3, kc3, vc3, low_ext):
    bsz, plen, _ = kc3.shape
    nb = plen // BLOCK
    nbt = STEP_BATCH
    nseq = nbt * C_HEADS
    nchain = STICK_GROUP * nseq
    seq_blk = lambda rows: pl.BlockSpec((nbt, rows, C_WIDTH), (lambda b, i: (b, i, 0)) if rows == BLOCK
                                        else (lambda b, i: (b, 0, 0)))
    return pl.pallas_call(
        functools.partial(_stick_kernel, nb),
        grid=(bsz // nbt, nb),
        in_specs=[seq_blk(BLOCK), seq_blk(plen), seq_blk(plen), _const_spec((BLOCK, 2 * BLOCK))],
        out_specs=seq_blk(BLOCK),
        out_shape=jax.ShapeDtypeStruct((bsz, plen, C_WIDTH), _BF),
        scratch_shapes=[pltpu.VMEM((2, nchain, BLOCK, BLOCK), _F32),
                        pltpu.VMEM((2, nchain, BLOCK, BLOCK), _BF),
                        pltpu.VMEM((nseq, BLOCK, BLOCK), _F32),
                        pltpu.VMEM((nseq, BLOCK, BLOCK), _F32)],
        compiler_params=pltpu.CompilerParams(dimension_semantics=("arbitrary", "arbitrary"),
                                             vmem_limit_bytes=VMEM_LIMIT),
        name="stick",
    )(qc3, kc3, vc3, low_ext).reshape(bsz * plen, C_WIDTH)


def _s5_kernel(bsz, u_ref, wbu_ref, a_ref, wc_ref, d_ref, wglu_ref, y_ref, bur_ref, bui_ref, st_ref):
    c = pl.program_id(0)
    slab_w = S5_LANES // S5_SLABS
    ch_w = B_WIDTH // S5_SLABS

    @pl.when(c == 0)
    def _():
        st_ref[...] = jnp.zeros_like(st_ref)

    u = u_ref[...]
    ys = []
    for k in range(S5_SLABS):
        ls = slice(k * slab_w, (k + 1) * slab_w)
        o = jnp.dot(u[:, k * ch_w:(k + 1) * ch_w], wbu_ref[k], preferred_element_type=_F32)
        bur_ref[:, ls] = o[:, :slab_w]
        bui_ref[:, ls] = o[:, slab_w:]
        ar = jnp.broadcast_to(a_ref[0:1, ls], (bsz, slab_w))
        ai = jnp.broadcast_to(a_ref[1:2, ls], (bsz, slab_w))
        hr, hi = st_ref[0, :, ls], st_ref[1, :, ls]
        for t in range(S5_CHUNK):
            rows = slice(t * bsz, (t + 1) * bsz)
            hr, hi = (ar * hr - ai * hi + bur_ref[rows, ls], ar * hi + ai * hr + bui_ref[rows, ls])
            bur_ref[rows, ls] = hr
            bui_ref[rows, ls] = hi
        st_ref[0, :, ls] = hr
        st_ref[1, :, ls] = hi
        hcat = jnp.concatenate([bur_ref[:, ls].astype(_BF), bui_ref[:, ls].astype(_BF)], axis=1)
        ys.append(jnp.dot(hcat, wc_ref[k], preferred_element_type=_F32))
    y = jnp.concatenate(ys, axis=1) + d_ref[...] * u.astype(_F32)
    y = 0.5 * y * (1.0 + jnp.tanh(math.sqrt(2.0 / math.pi) * (y + 0.044715 * (y * y * y))))
    gate = jnp.dot(y.astype(_BF), wglu_ref[...], preferred_element_type=_F32)
    y_ref[...] = (y * (1.0 / (1.0 + jnp.exp(-gate)))).astype(_BF)


def _s5(bsz, u_tb, wbu, a_pack, wc, d_skip, w_glu):
    rows = u_tb.shape[0]
    rc = S5_CHUNK * bsz
    return pl.pallas_call(
        functools.partial(_s5_kernel, bsz),
        grid=(rows // rc,),
        in_specs=[
            pl.BlockSpec((rc, B_WIDTH), lambda c: (c, 0)),
            _const_spec((S5_SLABS, B_WIDTH // S5_SLABS, 2 * S5_LANES // S5_SLABS)),
            _const_spec((2, S5_LANES)),
            _const_spec((S5_SLABS, 2 * S5_LANES // S5_SLABS, B_WIDTH // S5_SLABS)),
            _const_spec((1, B_WIDTH)),
            _const_spec((B_WIDTH, B_WIDTH)),
        ],
        out_specs=pl.BlockSpec((rc, B_WIDTH), lambda c: (c, 0)),
        out_shape=jax.ShapeDtypeStruct((rows, B_WIDTH), _BF),
        scratch_shapes=[pltpu.VMEM((rc, S5_LANES), _F32), pltpu.VMEM((rc, S5_LANES), _F32),
                        pltpu.VMEM((2, bsz, S5_LANES), _F32)],
        compiler_params=pltpu.CompilerParams(dimension_semantics=("arbitrary",),
                                             vmem_limit_bytes=VMEM_LIMIT),
        name="s5",
    )(u_tb, wbu, a_pack, wc, d_skip, w_glu)


def _mix_mlp_kernel(plen, final, h_ref, ya_ref, yb_ref, yc_ref, ga_ref, gb_ref, gc_ref, wo_ref,
                    g2_ref, w1_ref, w2_ref, gf_ref, o_ref):
    tm = h_ref.shape[0]
    y = jnp.concatenate([_rms(ya_ref[...].astype(_F32), ga_ref[...]),
                         _rms(yb_ref[...].astype(_F32), gb_ref[...]),
                         _rms(yc_ref[...].astype(_F32), gc_ref[...])], axis=1).astype(_BF)
    h = h_ref[...] + jnp.dot(y, wo_ref[...], preferred_element_type=_F32)
    hn = _rms(h, g2_ref[...]).astype(_BF)
    ffc = D_FF // 4
    for cc in range(4):
        a = jnp.dot(hn, w1_ref[:, cc * ffc:(cc + 1) * ffc], preferred_element_type=_F32)
        a = jnp.square(jnp.maximum(a, 0.0)).astype(_BF)
        h = h + jnp.dot(a, w2_ref[cc * ffc:(cc + 1) * ffc, :], preferred_element_type=_F32)
    if final:
        o_ref[...] = _rms(h, gf_ref[...])
    else:
        r = (pl.program_id(0) * tm + lax.broadcasted_iota(jnp.int32, (tm, 1), 0)).astype(_F32)
        pos = r - jnp.floor((r + 0.5) * (1.0 / plen)) * plen
        o_ref[...] = jnp.where(pos >= PAD, h, 0.0)


def _mix_mlp(plen, final, h2d, ya, yb, yc, ga, gb, gc, wo, g2, w1, w2, gf):
    rows = h2d.shape[0]
    tm = ROW_TILE
    if final:
        seq = plen - BLOCK
        tiles = seq // tm
        grid = (rows // plen, tiles)
        row = lambda w: pl.BlockSpec((pl.Element(tm), pl.Element(w)),
                                     lambda b, j: (pl.multiple_of(b * plen + BLOCK + tm * j, BLOCK), 0))
        out_spec = pl.BlockSpec((tm, D_MODEL), lambda b, j: (b * tiles + j, 0))
        out_rows = (rows // plen) * seq
    else:
        grid = (rows // tm,)
        row = lambda w: pl.BlockSpec((tm, w), lambda i: (i, 0))
        out_spec, out_rows = row(D_MODEL), rows
    return pl.pallas_call(
        functools.partial(_mix_mlp_kernel, plen, final),
        grid=grid,
        in_specs=[row(D_MODEL), row(A_WIDTH), row(B_WIDTH), row(C_WIDTH),
                  _const_spec((1, A_WIDTH)), _const_spec((1, B_WIDTH)), _const_spec((1, C_WIDTH)),
                  _const_spec((D_MODEL, D_MODEL)), _const_spec((1, D_MODEL)),
                  _const_spec((D_MODEL, D_FF)), _const_spec((D_FF, D_MODEL)), _const_spec((1, D_MODEL))],
        out_specs=out_spec,
        out_shape=jax.ShapeDtypeStruct((out_rows, D_MODEL), _F32),
        compiler_params=pltpu.CompilerParams(dimension_semantics=("arbitrary",) * len(grid),
                                             vmem_limit_bytes=VMEM_LIMIT),
        name="mix_mlp",
    )(h2d, ya, yb, yc, ga, gb, gc, wo, g2, w1, w2, gf)


def _t5_bucket_tiles():
    max_exact = NUM_BUCKETS // 2
    r = jnp.arange(BLOCK)[:, None]
    c = jnp.arange(BLOCK)[None, :]
    tiles = []
    for dd in range(2):
        n = jnp.maximum(dd * BLOCK + r - c, 0)
        nf = jnp.maximum(n, max_exact).astype(_F32)
        large = max_exact + jnp.floor(jnp.log(nf / max_exact) / math.log(MAX_DISTANCE / max_exact)
                                      * (NUM_BUCKETS - max_exact)).astype(jnp.int32)
        large = jnp.minimum(large, NUM_BUCKETS - 1)
        tiles.append(jnp.where(n < max_exact, n, large))
    return jnp.stack(tiles).astype(jnp.int32)


def _pack_w_in(w_in):
    sizes = (A_WIDTH, KV_RANK, IDX_HEADS * IDX_DIM, IDX_DIM, IDX_HEADS, B_WIDTH, C_WIDTH, C_WIDTH, C_WIDTH)
    offs = np.cumsum((0,) + sizes)
    qa, ckv, iq, ik, iw, ub, qc, kc, vc = [w_in[:, offs[n]:offs[n + 1]] for n in range(len(sizes))]
    zeros = lambda n: jnp.zeros((w_in.shape[0], n), w_in.dtype)
    cols = [qa, ckv, iq, ik, iw, zeros(128 - IDX_DIM - IDX_HEADS), ub, qc, kc, vc]
    return jnp.concatenate(cols, axis=1).astype(_BF)


def _block_diag(blocks):
    n, r, c = blocks.shape
    eye = jnp.eye(n, dtype=blocks.dtype)
    return (eye[:, None, :, None] * blocks[:, :, None, :]).reshape(n * r, n * c)


def _s5_params(lam_re, lam_im, log_dt, b_re, b_im, c_re, c_im):
    dt = jnp.exp(log_dt)[:, None]
    mag = jnp.exp(lam_re * dt)
    ar = mag * jnp.cos(lam_im * dt)
    ai = mag * jnp.sin(lam_im * dt)
    den = lam_re * lam_re + lam_im * lam_im
    nr, ni = ar - 1.0, ai
    fr = (nr * lam_re + ni * lam_im) / den
    fi = (ni * lam_re - nr * lam_im) / den
    bfr = fr[:, :, None] * b_re - fi[:, :, None] * b_im
    bfi = fr[:, :, None] * b_im + fi[:, :, None] * b_re
    gps = S5_GROUPS // S5_SLABS
    wbu, wc = [], []
    for k in range(S5_SLABS):
        gs = slice(k * gps, (k + 1) * gps)
        wbu.append(jnp.concatenate([_block_diag(jnp.swapaxes(bfr[gs], 1, 2)),
                                    _block_diag(jnp.swapaxes(bfi[gs], 1, 2))], axis=1))
        wc.append(jnp.concatenate([_block_diag(jnp.swapaxes(c_re[gs], 1, 2)),
                                   _block_diag(jnp.swapaxes(-c_im[gs], 1, 2))], axis=0))
    a_pack = jnp.stack([ar.reshape(-1), ai.reshape(-1)])
    return jnp.stack(wbu).astype(_BF), a_pack, jnp.stack(wc).astype(_BF)


def kernel(x, meta_tokens, rel_bias, norm1_g, w_in, kv_norm_g, w_uk, w_uv, lambda_re, lambda_im, log_dt,
           b_re, b_im, c_re, c_im, d_skip, w_glu, gn_a, gn_b, gn_c, w_out, norm2_g, w_mlp1, w_mlp2, final_g):
    bsz, seq, _ = x.shape
    depth = w_in.shape[0]
    n_keys = seq + N_META
    k_top = min(TOPK_MAX, n_keys // 4)
    plen = n_keys + PAD
    rows = bsz * plen
    assert seq % ROW_TILE == 0 and rows % ROW_TILE == 0 and plen % S5_CHUNK == 0 and bsz % 8 == 0
    assert bsz % STEP_BATCH == 0

    head = jnp.concatenate([jnp.zeros((PAD, D_MODEL), x.dtype), meta_tokens.astype(x.dtype)], axis=0)
    h = jnp.concatenate([jnp.broadcast_to(head[None], (bsz, BLOCK, D_MODEL)), x], axis=1).reshape(rows, D_MODEL)

    bucket_t = jnp.swapaxes(_t5_bucket_tiles(), 1, 2)
    ar_ = jnp.arange(BLOCK)
    lower = (ar_[:, None] > ar_[None, :]).astype(_BF)
    ones = jnp.ones((BLOCK, BLOCK), _BF)
    low_ext = jnp.concatenate([lower, ones], axis=1)
    row1 = lambda v: v.reshape(1, -1).astype(_F32)

    for l in range(depth):
        w_all = _pack_w_in(w_in[l])
        wuk_bd = _block_diag(jnp.transpose(w_uk[l], (1, 2, 0))).astype(_BF)
        wuvt_pad = jnp.stack([jnp.pad(w_uv[l][:, hh, :].T, ((hh * HEAD_DIM, A_WIDTH - (hh + 1) * HEAD_DIM), (0, 0)))
                              for hh in range(A_HEADS)]).astype(_BF)
        qlat4, ckv, iq4, ikw, ub, qc, kc, vc = _inproj(h, row1(norm1_g[l]), w_all, row1(kv_norm_g[l]), wuk_bd)

        ckv3 = ckv.reshape(bsz, plen, KV_RANK)
        gk = PV_GROUP * BLOCK
        ckvt3 = jnp.swapaxes(jnp.pad(ckv3, ((0, 0), (0, -plen % gk), (0, 0))).reshape(bsz, -1, gk, KV_RANK), 2, 3)
        ya = _dsa(k_top, rel_bias.astype(_F32), qlat4, iq4, ikw.reshape(bsz, plen, 128),
                  ckv3, ckvt3, bucket_t, wuvt_pad, lower)

        wbu, a_pack, wc = _s5_params(lambda_re[l], lambda_im[l], log_dt[l], b_re[l], b_im[l], c_re[l], c_im[l])
        u_tb = jnp.swapaxes(ub.reshape(bsz, plen, B_WIDTH), 0, 1).reshape(rows, B_WIDTH)
        yb_tb = _s5(bsz, u_tb, wbu, a_pack, wc, row1(d_skip[l]), w_glu[l].astype(_BF))
        yb = jnp.swapaxes(yb_tb.reshape(plen, bsz, B_WIDTH), 0, 1).reshape(rows, B_WIDTH)

        yc = _stick(qc.reshape(bsz, plen, C_WIDTH), kc.reshape(bsz, plen, C_WIDTH),
                    vc.reshape(bsz, plen, C_WIDTH), low_ext)

        h = _mix_mlp(plen, l == depth - 1, h, ya, yb, yc, row1(gn_a[l]), row1(gn_b[l]), row1(gn_c[l]),
                     w_out[l].astype(_BF), row1(norm2_g[l]), w_mlp1[l].astype(_BF), w_mlp2[l].astype(_BF),
                     row1(final_g))

    return h.reshape(bsz, seq, D_MODEL)
```
